```python
import jax
import jax.numpy as jnp
from jax import lax
import numpy as np

D_MODEL = 1024
BATCH = 4
SEQ = 8192
DEPTH = 2

N_A_LAYERS = DEPTH // 2
N_B_LAYERS = DEPTH - N_A_LAYERS
HEAD_DIM = 64
N_HEADS_A = D_MODEL // HEAD_DIM
WIDTH_A = N_HEADS_A * HEAD_DIM
DECAY_LORA = 64
ICLR_LORA = 64
N_HEADS_B = D_MODEL // HEAD_DIM
WIDTH_B = N_HEADS_B * HEAD_DIM
MOBA_BLOCK = 256
MOBA_TOPK = 3
QUERY_CHUNK = 16
ROPE_THETA = 500000.0
ROPE_DIM = HEAD_DIM // 4
RMS_EPS = 1e-6
GN_EPS = 64e-5
NEG = -1e30

kernel_name = "yoco_rwkv7_moba_hybrid"


def rms_norm(x, g):
    xf = x.astype(jnp.float32)
    y = xf * lax.rsqrt(jnp.mean(xf * xf, axis=-1, keepdims=True) + RMS_EPS)
    return (y * g.astype(jnp.float32)).astype(x.dtype)


def rope_tables(T):
    inv_freq = jnp.power(jnp.float32(ROPE_THETA), -jnp.arange(0, ROPE_DIM, 2, dtype=jnp.float32) / ROPE_DIM)
    ang = jnp.arange(T, dtype=jnp.float32)[:, None] * inv_freq[None, :]
    return jnp.cos(ang), jnp.sin(ang)


def partial_rope(x, cos, sin):
    half = ROPE_DIM // 2
    xr = x[..., :ROPE_DIM].astype(jnp.float32)
    x1, x2 = xr[..., :half], xr[..., half:]
    rot = jnp.concatenate([x1 * cos - x2 * sin, x2 * cos + x1 * sin], axis=-1)
    return jnp.concatenate([rot.astype(x.dtype), x[..., ROPE_DIM:]], axis=-1)


def rwkv7_time_mix(h, mu, w_in, w0, w2, a0, a2, k_k, k_a, r_k, gn_w, gn_b, w_out):
    B, T, D = h.shape
    H, N = N_HEADS_A, HEAD_DIM
    xx = jnp.pad(h, ((0, 0), (1, 0), (0, 0)))[:, :-1] - h
    widths = (WIDTH_A, DECAY_LORA, WIDTH_A, WIDTH_A, ICLR_LORA, WIDTH_A)
    outs = []
    off = 0
    for i, wd in enumerate(widths):
        outs.append((h + xx * mu[i]) @ w_in[:, off:off + wd])
        off += wd
    r, wdn, k, v, adn, gp = [o.astype(jnp.float32) for o in outs]
    w = -jax.nn.softplus(-(w0 + jnp.tanh(wdn) @ w2)) - 0.5
    decay = jnp.exp(-jnp.exp(w))
    a = jax.nn.sigmoid(a0 + adn @ a2)
    kk = (k * k_k).reshape(B, T, H, N)
    kk = kk / jnp.maximum(jnp.linalg.norm(kk, axis=-1, keepdims=True), 1e-12)
    k = k * (1.0 + (a - 1.0) * k_a)

    def heads_tm(z):
        return z.reshape(B, T, H, N).transpose(1, 0, 2, 3)

    r4, d4, k4, v4 = heads_tm(r), heads_tm(decay), heads_tm(k), heads_tm(v)
    kk4 = heads_tm(kk)
    b4 = kk4 * heads_tm(a)

    def step(S, inp):
        r_t, d_t, k_t, v_t, kk_t, b_t = inp
        sa = jnp.einsum('bhvk,bhk->bhv', S, kk_t)
        S = S * d_t[:, :, None, :] - sa[..., None] * b_t[:, :, None, :] + v_t[..., None] * k_t[:, :, None, :]
        y = jnp.einsum('bhvk,bhk->bhv', S, r_t)
        return S, y

    S0 = jnp.zeros((B, H, N, N), jnp.float32)
    _, y = lax.scan(step, S0, (r4, d4, k4, v4, kk4, b4))
    y = y.transpose(1, 0, 2, 3)
    mean = jnp.mean(y, axis=-1, keepdims=True)
    var = jnp.mean(jnp.square(y - mean), axis=-1, keepdims=True)
    y = ((y - mean) * lax.rsqrt(var + GN_EPS)).reshape(B, T, D) * gn_w + gn_b
    bonus = jnp.sum(r.reshape(B, T, H, N) * k.reshape(B, T, H, N) * r_k, axis=-1, keepdims=True) * v.reshape(B, T, H, N)
    y = (y + bonus.reshape(B, T, D)) * jax.nn.silu(gp)
    return y.astype(h.dtype) @ w_out


def shared_kv(x, norm_kv, w_kv, k_norm, cos, sin):
    B, T, D = x.shape
    kv = rms_norm(x, norm_kv) @ w_kv
    k = kv[..., :WIDTH_B].reshape(B, T, N_HEADS_B, HEAD_DIM).transpose(0, 2, 1, 3)
    v = kv[..., WIDTH_B:].reshape(B, T, N_HEADS_B, HEAD_DIM).transpose(0, 2, 1, 3)
    k = partial_rope(rms_norm(k, k_norm), cos, sin)
    nb = -(-T // MOBA_BLOCK)
    pad = nb * MOBA_BLOCK - T
    k = jnp.pad(k, ((0, 0), (0, 0), (0, pad), (0, 0)))
    v = jnp.pad(v, ((0, 0), (0, 0), (0, pad), (0, 0)))
    kb = k.reshape(B, N_HEADS_B, nb, MOBA_BLOCK, HEAD_DIM)
    vb = v.reshape(B, N_HEADS_B, nb, MOBA_BLOCK, HEAD_DIM)
    kmean = jnp.mean(kb.astype(jnp.float32), axis=3)
    return kb, vb, kmean


def moba_attention(h, w_in, q_norm, w_out, kb, vb, kmean, cos, sin):
    B, T, D = h.shape
    H, hd, C, BS = N_HEADS_B, HEAD_DIM, QUERY_CHUNK, MOBA_BLOCK
    nb = kb.shape[2]
    topk = min(MOBA_TOPK, nb)
    scale = 1.0 / float(np.sqrt(hd))
    proj = h @ w_in
    q = proj[..., :WIDTH_B].reshape(B, T, H, hd).transpose(0, 2, 1, 3)
    gate = proj[..., WIDTH_B:]
    q = partial_rope(rms_norm(q, q_norm), cos, sin)
    nc = T // C
    qch = q.reshape(B, H, nc, C, hd).transpose(2, 0, 1, 3, 4)
    bi = jnp.arange(B)[:, None, None, None]
    hi = jnp.arange(H)[None, :, None, None]

    def chunk_fn(args):
        qc, ci = args
        cb = (ci * C) // BS
        qpos = ci * C + jnp.arange(C)
        gs = jnp.einsum('bhcd,bhnd->bhcn', qc.astype(jnp.float32), kmean)
        gs = jnp.where(jnp.arange(nb) < cb, gs, NEG)
        _, idx = lax.top_k(gs, topk)
        valid = idx < cb
        ksel = kb[bi, hi, idx]
        vsel = vb[bi, hi, idx]
        s_sel = jnp.einsum('bhcd,bhcjsd->bhcjs', qc, ksel).astype(jnp.float32) * scale
        s_sel = jnp.where(valid[..., None], s_sel, NEG).reshape(B, H, C, topk * BS)
        kown = lax.dynamic_index_in_dim(kb, cb, axis=2, keepdims=False)
        vown = lax.dynamic_index_in_dim(vb, cb, axis=2, keepdims=False)
        s_own = jnp.einsum('bhcd,bhsd->bhcs', qc, kown).astype(jnp.float32) * scale
        kpos = cb * BS + jnp.arange(BS)
        s_own = jnp.where(kpos[None, :] <= qpos[:, None], s_own, NEG)
        p = jax.nn.softmax(jnp.concatenate([s_sel, s_own], axis=-1), axis=-1)
        p_sel = p[..., :topk * BS].reshape(B, H, C, topk, BS).astype(vb.dtype)
        p_own = p[..., topk * BS:].astype(vb.dtype)
        return jnp.einsum('bhcjs,bhcjsd->bhcd', p_sel, vsel) + jnp.einsum('bhcs,bhsd->bhcd', p_own, vown)

    o = lax.map(chunk_fn, (qch, jnp.arange(nc)))
    o = o.transpose(1, 0, 3, 2, 4).reshape(B, T, WIDTH_B)
    return (o * jax.nn.silu(gate)) @ w_out


def setup_inputs(seed: int = 0) -> dict:
    key = jax.random.key(seed)
    ks = jax.random.split(key, 24)
    D, NA, NB_ = D_MODEL, N_A_LAYERS, N_B_LAYERS
    in_a = 4 * WIDTH_A + DECAY_LORA + ICLR_LORA
    nrm = jax.random.normal
    return {
        "x": nrm(ks[0], (BATCH, SEQ, D), jnp.float32),
        "norm_a": 1.0 + 0.02 * nrm(ks[1], (NA, D), jnp.float32),
        "mu_a": jax.random.uniform(ks[2], (NA, 6, D), jnp.float32),
        "w_in_a": nrm(ks[3], (NA, D, in_a), jnp.float32) * D ** -0.5,
        "w0_a": jax.random.uniform(ks[4], (NA, D), jnp.float32, -5.0, 0.0),
        "w2_a": nrm(ks[5], (NA, DECAY_LORA, D), jnp.float32) * 0.5 * DECAY_LORA ** -0.5,
        "a0_a": 0.1 * nrm(ks[6], (NA, D), jnp.float32),
        "a2_a": nrm(ks[7], (NA, ICLR_LORA, D), jnp.float32) * 0.5 * ICLR_LORA ** -0.5,
        "kk_a": 0.85 + 0.05 * nrm(ks[8], (NA, D), jnp.float32),
        "ka_a": 1.0 + 0.05 * nrm(ks[9], (NA, D), jnp.float32),
        "rk_a": 0.1 * nrm(ks[10], (NA, N_HEADS_A, HEAD_DIM), jnp.float32),
        "gn_w_a": 1.0 + 0.02 * nrm(ks[11], (NA, D), jnp.float32),
        "gn_b_a": 0.02 * nrm(ks[12], (NA, D), jnp.float32),
        "w_out_a": nrm(ks[13], (NA, WIDTH_A, D), jnp.float32) * WIDTH_A ** -0.5,
        "norm_kv": 1.0 + 0.02 * nrm(ks[14], (D,), jnp.float32),
        "w_kv": nrm(ks[15], (D, 2 * WIDTH_B), jnp.float32) * D ** -0.5,
        "k_norm": 1.0 + 0.02 * nrm(ks[16], (HEAD_DIM,), jnp.float32),
        "norm_b": 1.0 + 0.02 * nrm(ks[17], (NB_, D), jnp.float32),
        "w_in_b": nrm(ks[18], (NB_, D, 2 * WIDTH_B), jnp.float32) * D ** -0.5,
        "q_norm_b": 1.0 + 0.02 * nrm(ks[19], (NB_, HEAD_DIM), jnp.float32),
        "w_out_b": nrm(ks[20], (NB_, WIDTH_B, D), jnp.float32) * WIDTH_B ** -0.5,
    }


def reference(x, norm_a, mu_a, w_in_a, w0_a, w2_a, a0_a, a2_a, kk_a, ka_a, rk_a, gn_w_a, gn_b_a, w_out_a,
              norm_kv, w_kv, k_norm, norm_b, w_in_b, q_norm_b, w_out_b):
    T = x.shape[1]
    cos, sin = rope_tables(T)
    kb = vb = kmean = None
    for layer in range(DEPTH):
        if layer < N_A_LAYERS:
            i = layer
            x = x + rwkv7_time_mix(rms_norm(x, norm_a[i]), mu_a[i], w_in_a[i], w0_a[i], w2_a[i], a0_a[i], a2_a[i],
                                   kk_a[i], ka_a[i], rk_a[i], gn_w_a[i], gn_b_a[i], w_out_a[i])
        else:
            if layer == N_A_LAYERS:
                kb, vb, kmean = shared_kv(x, norm_kv, w_kv, k_norm, cos, sin)
            j = layer - N_A_LAYERS
            x = x + moba_attention(rms_norm(x, norm_b[j]), w_in_b[j], q_norm_b[j], w_out_b[j], kb, vb, kmean, cos, sin)
    return x
```

```python
import functools

import jax
import jax.numpy as jnp
from jax import lax
from jax.experimental import pallas as pl
from jax.experimental.pallas import tpu as pltpu

F32 = jnp.float32
BF16 = jnp.bfloat16

D_MODEL = 1024
HEAD_DIM = 64
N_HEADS = D_MODEL // HEAD_DIM
LANES = 128
N_PAIRS = D_MODEL // LANES
LORA = 64
MOBA_BLOCK = 256
MOBA_TOPK = 3
ROPE_THETA = 500000.0
ROPE_DIM = HEAD_DIM // 4
ROPE_HALF = ROPE_DIM // 2
RMS_EPS = 1e-6
GN_EPS = 64e-5
NEG = -1e30
CHUNK = 64
VMEM_LIMIT = 56 * 1024 * 1024

NN = (((1,), (0,)), ((), ()))
NT = (((1,), (1,)), ((), ()))


def _mm(a, b, dims=NN):
    return lax.dot_general(a, b, dims, preferred_element_type=F32)


def _split(x):
    hi = x.astype(BF16)
    lo = (x - hi.astype(F32)).astype(BF16)
    return hi, lo


def _mm3(a, b, dims=NN):
    ah, al = _split(a)
    bh, bl = _split(b)
    return _mm(ah, bh, dims) + (_mm(ah, bl, dims) + _mm(al, bh, dims))


def _mm_exact_rhs(a, b_bf16, dims=NN):
    ah, al = _split(a)
    return _mm(ah, b_bf16, dims) + _mm(al, b_bf16, dims)


def _head_ones():
    r = lax.broadcasted_iota(jnp.int32, (LANES, LANES), 0) // HEAD_DIM
    c = lax.broadcasted_iota(jnp.int32, (LANES, LANES), 1) // HEAD_DIM
    return jnp.where(r == c, 1.0, 0.0).astype(BF16)


def _head_sum(x, ones):
    parts = [_mm_exact_rhs(x[:, p * LANES:(p + 1) * LANES], ones) for p in range(x.shape[1] // LANES)]
    return jnp.concatenate(parts, axis=1)


def _sigmoid(x):
    return 1.0 / (1.0 + jnp.exp(-x))


def _rms(x, g):
    return x * lax.rsqrt(jnp.mean(x * x, axis=-1, keepdims=True) + RMS_EPS) * g


def _params(*sem):
    return pltpu.CompilerParams(dimension_semantics=sem, vmem_limit_bytes=VMEM_LIMIT)


def _full(shape):
    return pl.BlockSpec(shape, lambda *_: (0,) * len(shape))


def _rwkv_pre_body(x_ref, xp_ref, na_ref, mu_ref, wr_ref, wwd_ref, wk_ref, wv_ref, wad_ref, wg_ref,
                   w0_ref, w2_ref, a0_ref, a2_ref, kkw_ref, kaw_ref, rk_ref,
                   r_ref, ld_ref, k_ref, v_ref, kk_ref, nb_ref, bonus_ref, g_ref):
    x = x_ref[0]
    gn = na_ref[...]
    h = _rms(x, gn)
    hp = _rms(xp_ref[0], gn)
    prev = jnp.where(pl.program_id(1) == 0, 0.0, hp[7:8, :])
    rolled = pltpu.roll(h, 1, axis=0)
    first = lax.broadcasted_iota(jnp.int32, h.shape, 0) == 0
    xx = jnp.where(first, prev, rolled) - h

    def mix(i):
        return (h + xx * mu_ref[i:i + 1, :]).astype(BF16)

    r = _mm(mix(0), wr_ref[...])
    wdn = _mm(mix(1), wwd_ref[...])
    k = _mm(mix(2), wk_ref[...])
    v = _mm(mix(3), wv_ref[...])
    adn = _mm(mix(4), wad_ref[...])
    gp = _mm(mix(5), wg_ref[...])

    z = -(w0_ref[...] + _mm(jnp.tanh(wdn).astype(BF16), w2_ref[...]))
    softplus = jnp.maximum(z, 0.0) + jnp.log(1.0 + jnp.exp(-jnp.abs(z)))
    ld_ref[0] = -jnp.exp(-softplus - 0.5)
    a = _sigmoid(a0_ref[...] + _mm(adn.astype(BF16), a2_ref[...]))

    ones = _head_ones()
    kk = k * kkw_ref[...]
    kk = kk / jnp.maximum(jnp.sqrt(_head_sum(kk * kk, ones)), 1e-12)
    k2 = k * (1.0 + (a - 1.0) * kaw_ref[...])
    r_ref[0] = r
    k_ref[0] = k2
    v_ref[0] = v
    kk_ref[0] = kk
    nb_ref[0] = -(kk * a)
    bonus_ref[0] = _head_sum(r * k2 * rk_ref[...], ones) * v
    g_ref[0] = gp * _sigmoid(gp)


def _rwkv_pre(x, norm_a, mu, w_in, w0, w2, a0, a2, kkw, kaw, rk, tm=256):
    B, T, D = x.shape
    offs = [0, D, D + LORA, 2 * D + LORA, 3 * D + LORA, 3 * D + 2 * LORA, 4 * D + 2 * LORA]
    ws = [w_in[:, offs[i]:offs[i + 1]].astype(BF16) for i in range(6)]
    row = lambda a: a.reshape(1, D)
    big = pl.BlockSpec((1, tm, D), lambda b, i: (b, i, 0))
    prev = pl.BlockSpec((1, 8, D), lambda b, i: (b, jnp.maximum(i * (tm // 8) - 1, 0), 0))
    ins = [x, x, row(norm_a), mu] + ws + [row(w0), w2.astype(BF16), row(a0), a2.astype(BF16),
                                          row(kkw), row(kaw), row(rk)]
    in_specs = [big, prev] + [_full(a.shape) for a in ins[2:]]
    out = jax.ShapeDtypeStruct((B, T, D), F32)
    return pl.pallas_call(
        _rwkv_pre_body, grid=(B, T // tm), in_specs=in_specs, out_specs=[big] * 8, out_shape=[out] * 8,
        compiler_params=_params("parallel", "parallel"), name="rwkv_pre")(*ins)


def _stack(x, lane_lo):
    return jnp.concatenate([jnp.where(lane_lo, x, 0.0), jnp.where(lane_lo, 0.0, x)], axis=0)


def _unstack(z):
    half = z.shape[0] // 2
    return z[:half] + z[half:]


def _rwkv_chunk_body(r_ref, ld_ref, k_ref, v_ref, kk_ref, nb_ref, qh_ref, yi_ref, g_ref, c_ref, *, nc):
    L = CHUNK
    S = 2 * L
    lane_lo = lax.broadcasted_iota(jnp.int32, (L, LANES), 1) < HEAD_DIM
    ti = lax.broadcasted_iota(jnp.int32, (L, L), 0)
    tj = lax.broadcasted_iota(jnp.int32, (L, L), 1)
    tri = jnp.where(tj <= ti, 1.0, 0.0).astype(BF16)
    pi = lax.broadcasted_iota(jnp.int32, (2 * S, 2 * S), 0)
    pj = lax.broadcasted_iota(jnp.int32, (2 * S, 2 * S), 1)
    same_head = ((pi >> 6) & 1) == ((pj >> 6) & 1)
    t_i, s_j = pi & (L - 1), pj & (L - 1)
    valid = same_head & ((s_j < t_i) | ((pi >= S) & (s_j == t_i)))
    ei = lax.broadcasted_iota(jnp.int32, (S, S), 0)
    ej = lax.broadcasted_iota(jnp.int32, (S, S), 1)
    eye = ei == ej

    def chunk(c, carry):
        sl = pl.ds(pl.multiple_of(c * L, L), L)
        r, ld, k, v, kk, nb = (ref[0, sl, :] for ref in (r_ref, ld_ref, k_ref, v_ref, kk_ref, nb_ref))
        l1, l2 = _split(ld)
        l3 = (ld - l1.astype(F32) - l2.astype(F32)).astype(BF16)
        cum = _mm(tri, l1) + (_mm(tri, l2) + _mm(tri, l3))
        cref = cum[L // 2 - 1:L // 2, :]
        clast = cum[L - 1:L, :]
        ecum = cum - cref
        e_in = jnp.exp(ecum)
        e_ex = jnp.exp(ecum - ld)
        e_neg = jnp.exp(-ecum)
        p_ref = jnp.exp(cref)
        e_last = jnp.exp(clast - cref)
        q_m, q_n = kk * e_ex, r * e_in
        k_b, k_k = nb * e_neg, k * e_neg
        lhs = jnp.concatenate([_stack(q_m, lane_lo), _stack(q_n, lane_lo)], axis=0)
        rhs = jnp.concatenate([_stack(k_b, lane_lo), _stack(k_k, lane_lo)], axis=0)
        pm = jnp.where(valid, _mm3(lhs, rhs, NT), 0.0)
        m_b, m_k = pm[:S, :S], pm[:S, S:]
        n_b, n_k = pm[S:, :S], pm[S:, S:]
        tm = jnp.where(eye, 1.0, 0.0) + m_b
        xp = m_b
        for _ in range(5):
            xp = _mm3(xp, xp)
            tm = tm + _mm3(tm, xp)
        sv = _stack(v, lane_lo)
        wu = _mm3(tm, jnp.concatenate([_stack(q_m * p_ref, lane_lo), _mm3(m_k, sv)], axis=1))
        kbl_t = _stack(k_b * e_last, lane_lo).T
        kkl_t = _stack(k_k * e_last, lane_lo).T
        gc = _mm3(kbl_t, wu)
        g_bd = gc[:, :S] + jnp.where(eye, jnp.exp(clast), 0.0)
        c_bd = gc[:, S:] + _mm3(kkl_t, sv)
        nw = _mm3(n_b, wu)
        qh_ref[0, sl, :] = r * (e_in * p_ref) + _unstack(nw[:, :S])
        yi_ref[0, sl, :] = _unstack(nw[:, S:] + _mm3(n_k, sv))
        g_ref[0, 0, c] = _unstack(g_bd)
        c_ref[0, 0, c] = _unstack(c_bd)
        return carry

    lax.fori_loop(0, nc, chunk, 0)


def _rwkv_chunks(r, ld, k, v, kk, nb, nc=4):
    B, T, D = r.shape
    L = CHUNK
    tt = nc * L
    nch = T // L
    nat = pl.BlockSpec((1, tt, LANES), lambda b, p, i: (b, i, p))
    op = pl.BlockSpec((1, 1, nc, L, LANES), lambda b, p, i: (b, p, i, 0, 0))
    nat_shape = jax.ShapeDtypeStruct((B, T, D), F32)
    op_shape = jax.ShapeDtypeStruct((B, N_PAIRS, nch, L, LANES), F32)
    return pl.pallas_call(
        functools.partial(_rwkv_chunk_body, nc=nc), grid=(B, N_PAIRS, T // tt),
        in_specs=[nat] * 6, out_specs=[nat, nat, op, op], out_shape=[nat_shape, nat_shape, op_shape, op_shape],
        compiler_params=_params("parallel", "parallel", "parallel"), name="rwkv_chunks")(r, ld, k, v, kk, nb)


def _rwkv_state_body(qh_ref, yi_ref, g_ref, c_ref, y_ref, h_ref, *, nc):
    L = CHUNK
    lane_lo = lax.broadcasted_iota(jnp.int32, (L, LANES), 1) < HEAD_DIM

    @pl.when(pl.program_id(2) == 0)
    def _():
        h_ref[...] = jnp.zeros_like(h_ref)

    def chunk(c, h):
        sl = pl.ds(pl.multiple_of(c * L, L), L)
        y_ref[0, sl, :] = _mm3(qh_ref[0, sl, :], h) + yi_ref[0, sl, :]
        return _mm3(_stack(g_ref[0, 0, c], lane_lo), h) + _stack(c_ref[0, 0, c], lane_lo)

    h_ref[...] = lax.fori_loop(0, nc, chunk, h_ref[...])


def _rwkv_state(qh, yi, g, c, nc=8):
    B, T, D = qh.shape
    L = CHUNK
    tt = nc * L
    nat = pl.BlockSpec((1, tt, LANES), lambda b, p, i: (b, i, p))
    op = pl.BlockSpec((1, 1, nc, L, LANES), lambda b, p, i: (b, p, i, 0, 0))
    return pl.pallas_call(
        functools.partial(_rwkv_state_body, nc=nc), grid=(B, N_PAIRS, T // tt),
        in_specs=[nat, nat, op, op], out_specs=nat, out_shape=jax.ShapeDtypeStruct((B, T, D), F32),
        scratch_shapes=[pltpu.VMEM((2 * HEAD_DIM, LANES), F32)],
        compiler_params=_params("parallel", "parallel", "arbitrary"), name="rwkv_state")(qh, yi, g, c)


def _rwkv_post_body(x_ref, y_ref, bonus_ref, g_ref, gw_ref, gb_ref, wo_ref, o_ref):
    ones = _head_ones()
    y = y_ref[0]
    mean = _head_sum(y, ones) * (1.0 / HEAD_DIM)
    d = y - mean
    var = _head_sum(d * d, ones) * (1.0 / HEAD_DIM)
    yn = d * lax.rsqrt(var + GN_EPS) * gw_ref[...] + gb_ref[...]
    z = ((yn + bonus_ref[0]) * g_ref[0]).astype(BF16)
    o_ref[0] = x_ref[0] + _mm(z, wo_ref[...])


def _rwkv_post(x, y, bonus, g, gn_w, gn_b, w_out, tm=512):
    B, T, D = x.shape
    big = pl.BlockSpec((1, tm, D), lambda b, i: (b, i, 0))
    ins = [x, y, bonus, g, gn_w.reshape(1, D), gn_b.reshape(1, D), w_out.astype(BF16)]
    return pl.pallas_call(
        _rwkv_post_body, grid=(B, T // tm), in_specs=[big] * 4 + [_full(a.shape) for a in ins[4:]],
        out_specs=big, out_shape=jax.ShapeDtypeStruct((B, T, D), F32),
        compiler_params=_params("parallel", "parallel"), name="rwkv_post")(*ins)


def _head_norm_rope_t(xt, gain, cos, sin):
    heads = []
    for h in range(N_HEADS):
        xh = xt[h * HEAD_DIM:(h + 1) * HEAD_DIM, :]
        xh = xh * lax.rsqrt(jnp.mean(xh * xh, axis=0, keepdims=True) + RMS_EPS) * gain
        x1, x2 = xh[:ROPE_HALF], xh[ROPE_HALF:ROPE_DIM]
        heads += [x1 * cos - x2 * sin, x2 * cos + x1 * sin, xh[ROPE_DIM:]]
    return jnp.concatenate(heads, axis=0)


def _moba_pre_body(x_ref, nkv_ref, nb_ref, wkv_ref, wq_ref, wg_ref, kn_ref, qn_ref, cos_ref, sin_ref,
                   k_ref, km_ref, qt_ref, vt_ref, g_ref):
    x = x_ref[0]
    xn = x * lax.rsqrt(jnp.mean(x * x, axis=-1, keepdims=True) + RMS_EPS)
    h_kv = (xn * nkv_ref[...]).astype(BF16)
    h_b = (xn * nb_ref[...]).astype(BF16)
    cos, sin = cos_ref[...], sin_ref[...]
    tile = lambda g: jnp.concatenate([g] * (x.shape[0] // LANES), axis=1)
    kvt = _mm(wkv_ref[...], h_kv, NT)
    vt_ref[0, 0] = kvt[D_MODEL:].astype(BF16)
    kt = _head_norm_rope_t(kvt[:D_MODEL], tile(kn_ref[...]), cos, sin)
    k = kt.T
    k_ref[0] = k.astype(BF16)
    km_ref[0, 0] = jnp.mean(k, axis=0, keepdims=True)
    qt = _head_norm_rope_t(_mm(wq_ref[...], h_b, NT), tile(qn_ref[...]), cos, sin)
    qt_ref[0, 0] = (qt * (1.0 / HEAD_DIM ** 0.5)).astype(BF16)
    gate = _mm(h_b, wg_ref[...])
    g_ref[0] = gate * _sigmoid(gate)


def _rope_tables_t(T):
    inv_freq = jnp.power(jnp.float32(ROPE_THETA), -jnp.arange(0, ROPE_DIM, 2, dtype=F32) / ROPE_DIM)
    ang = inv_freq[:, None] * jnp.arange(T, dtype=F32)[None, :]
    return jnp.cos(ang), jnp.sin(ang)


def _moba_pre(x, norm_kv, norm_b, w_kv, w_in, k_norm, q_norm):
    B, T, D = x.shape
    tm = MOBA_BLOCK
    nb = T // tm
    cos, sin = _rope_tables_t(T)
    bcast = lambda g: jnp.broadcast_to(g.reshape(HEAD_DIM, 1), (HEAD_DIM, LANES))
    ins = [x, norm_kv.reshape(1, D), norm_b.reshape(1, D), w_kv.T.astype(BF16), w_in[:, :D].T.astype(BF16),
           w_in[:, D:].astype(BF16), bcast(k_norm), bcast(q_norm), cos, sin]
    big = pl.BlockSpec((1, tm, D), lambda b, i: (b, i, 0))
    tab = pl.BlockSpec((ROPE_HALF, tm), lambda b, i: (0, i))
    tr = pl.BlockSpec((1, 1, D, tm), lambda b, i: (b, i, 0, 0))
    in_specs = [big] + [_full(a.shape) for a in ins[1:8]] + [tab, tab]
    out_specs = [big, pl.BlockSpec((1, 1, 1, D), lambda b, i: (b, i, 0, 0)), tr, tr, big]
    out_shape = [jax.ShapeDtypeStruct((B, T, D), BF16), jax.ShapeDtypeStruct((B, nb, 1, D), F32),
                 jax.ShapeDtypeStruct((B, nb, D, tm), BF16), jax.ShapeDtypeStruct((B, nb, D, tm), BF16),
                 jax.ShapeDtypeStruct((B, T, D), F32)]
    return pl.pallas_call(
        _moba_pre_body, grid=(B, nb), in_specs=in_specs, out_specs=out_specs, out_shape=out_shape,
        compiler_params=_params("parallel", "parallel"), name="moba_pre")(*ins)


def _moba_attn_body(k_ref, km_ref, qt_ref, vt_ref, o_ref, f_ref, *, nb):
    BS = MOBA_BLOCK
    i = pl.program_id(2)
    qt = qt_ref[0, 0]
    row_lo = lax.broadcasted_iota(jnp.int32, qt.shape, 0) < HEAD_DIM
    km_hi, km_lo = _split(km_ref[0])
    blk = lax.broadcasted_iota(jnp.int32, (nb, BS), 0)
    key_pos = lax.broadcasted_iota(jnp.int32, (BS, BS), 0)
    qry_pos = lax.broadcasted_iota(jnp.int32, (BS, BS), 1)
    outs = []
    for h in range(2):
        qh = jnp.where(row_lo if h == 0 else jnp.logical_not(row_lo), qt, jnp.zeros_like(qt))
        gs = _mm(km_hi, qh) + _mm(km_lo, qh)
        cnt = jnp.zeros((nb, BS), jnp.int32)
        for m in range(nb):
            gm = gs[m:m + 1, :]
            ahead = (gm > gs) | ((gm == gs) & (blk > m))
            cnt = cnt + jnp.where(ahead & (i > m), 1, 0)
        f_ref[h] = jnp.where((blk < i) & (cnt < MOBA_TOPK), 0.0, NEG)

        own = pl.ds(pl.multiple_of(i * BS, BS), BS)
        s = jnp.where(key_pos <= qry_pos, _mm(k_ref[0, own, :], qh), NEG)
        m0 = jnp.max(s, axis=0, keepdims=True)
        p = jnp.exp(s - m0)
        l0 = jnp.sum(p, axis=0, keepdims=True)
        vh = lambda j: vt_ref[0, j, h * HEAD_DIM:(h + 1) * HEAD_DIM, :]
        acc0 = _mm(vh(i), p.astype(BF16))

        def past(j, carry):
            m_run, l_run, acc = carry
            rows = pl.ds(pl.multiple_of(j * BS, BS), BS)
            s = _mm(k_ref[0, rows, :], qh) + f_ref[h, pl.ds(j, 1), :]
            m_new = jnp.maximum(m_run, jnp.max(s, axis=0, keepdims=True))
            alpha = jnp.exp(m_run - m_new)
            p = jnp.exp(s - m_new)
            l_new = alpha * l_run + jnp.sum(p, axis=0, keepdims=True)
            return m_new, l_new, alpha * acc + _mm(vh(j), p.astype(BF16))

        _, l_fin, acc = lax.fori_loop(0, i, past, (m0, l0, acc0))
        outs.append(acc / l_fin)
    o_ref[0] = jnp.concatenate(outs, axis=0).T


def _moba_attn(k, kmean, qt, vt):
    B, T, D = k.shape
    BS = MOBA_BLOCK
    nb = T // BS
    in_specs = [pl.BlockSpec((1, T, LANES), lambda b, p, i: (b, 0, p)),
                pl.BlockSpec((1, nb, LANES), lambda b, p, i: (b, 0, p)),
                pl.BlockSpec((1, 1, LANES, BS), lambda b, p, i: (b, i, p, 0)),
                pl.BlockSpec((1, nb, LANES, BS), lambda b, p, i: (b, 0, p, 0))]
    return pl.pallas_call(
        functools.partial(_moba_attn_body, nb=nb), grid=(B, N_PAIRS, nb), in_specs=in_specs,
        out_specs=pl.BlockSpec((1, BS, LANES), lambda b, p, i: (b, i, p)),
        out_shape=jax.ShapeDtypeStruct((B, T, D), F32),
        scratch_shapes=[pltpu.VMEM((2, nb, BS), F32)],
        compiler_params=_params("parallel", "parallel", "arbitrary"), name="moba_attn")(k, kmean, qt, vt)


def _moba_post_body(x_ref, o_ref, g_ref, wo_ref, out_ref):
    out_ref[0] = x_ref[0] + _mm((o_ref[0] * g_ref[0]).astype(BF16), wo_ref[...])


def _moba_post(x, o, g, w_out, tm=512):
    B, T, D = x.shape
    big = pl.BlockSpec((1, tm, D), lambda b, i: (b, i, 0))
    return pl.pallas_call(
        _moba_post_body, grid=(B, T // tm), in_specs=[big, big, big, _full((D, D))], out_specs=big,
        out_shape=jax.ShapeDtypeStruct((B, T, D), F32),
        compiler_params=_params("parallel", "parallel"), name="moba_post")(x, o, g, w_out.astype(BF16))


def _rwkv_layer(x, norm_a, mu, w_in, w0, w2, a0, a2, kkw, kaw, rk, gn_w, gn_b, w_out):
    r, ld, k, v, kk, nb, bonus, g = _rwkv_pre(x, norm_a, mu, w_in, w0, w2, a0, a2, kkw, kaw, rk.reshape(-1))
    qh, yi, gm, cm = _rwkv_chunks(r, ld, k, v, kk, nb)
    y = _rwkv_state(qh, yi, gm, cm)
    return _rwkv_post(x, y, bonus, g, gn_w, gn_b, w_out)


def _moba_layer(x, norm_kv, w_kv, k_norm, norm_b, w_in, q_norm, w_out):
    B, T, D = x.shape
    k, kmean, qt, vt, g = _moba_pre(x, norm_kv, norm_b, w_kv, w_in, k_norm, q_norm)
    o = _moba_attn(k, kmean.reshape(B, T // MOBA_BLOCK, D), qt, vt)
    return _moba_post(x, o, g, w_out)


def kernel(x, norm_a, mu_a, w_in_a, w0_a, w2_a, a0_a, a2_a, kk_a, ka_a, rk_a, gn_w_a, gn_b_a, w_out_a, norm_kv, w_kv, k_norm, norm_b, w_in_b, q_norm_b, w_out_b):
    x = _rwkv_layer(x, norm_a[0], mu_a[0], w_in_a[0], w0_a[0], w2_a[0], a0_a[0], a2_a[0], kk_a[0], ka_a[0],
                    rk_a[0], gn_w_a[0], gn_b_a[0], w_out_a[0])
    return _moba_layer(x, norm_kv, w_kv, k_norm, norm_b[0], w_in_b[0], q_norm_b[0], w_out_b[0])
```

```python
import functools

import jax
import jax.numpy as jnp
from jax import lax
from jax.experimental import pallas as pl
from jax.experimental.pallas import tpu as pltpu

F32 = jnp.float32
BF16 = jnp.bfloat16

D_MODEL = 1024
HEAD_DIM = 64
N_HEADS = D_MODEL // HEAD_DIM
LANES = 128
N_PAIRS = D_MODEL // LANES
LORA = 64
MOBA_BLOCK = 256
MOBA_TOPK = 3
ROPE_THETA = 500000.0
ROPE_DIM = HEAD_DIM // 4
ROPE_HALF = ROPE_DIM // 2
RMS_EPS = 1e-6
GN_EPS = 64e-5
NEG = -1e30
LOG2E = 1.4426950408889634
SUM_ROWS = 16
TAG_LANES = LANES - HEAD_DIM
CHUNK = 64
VMEM_LIMIT = 56 * 1024 * 1024

NN = (((1,), (0,)), ((), ()))
NT = (((1,), (1,)), ((), ()))


def _mm(a, b, dims=NN):
    return lax.dot_general(a, b, dims, preferred_element_type=F32)


def _split(x):
    hi = x.astype(BF16)
    lo = (x - hi.astype(F32)).astype(BF16)
    return hi, lo


def _bf(x):
    return x.astype(BF16)


def _mm_exact_rhs(a, b_bf16, dims=NN):
    ah, al = _split(a)
    return _mm(ah, b_bf16, dims) + _mm(al, b_bf16, dims)


def _head_ones():
    r = lax.broadcasted_iota(jnp.int32, (LANES, LANES), 0) // HEAD_DIM
    c = lax.broadcasted_iota(jnp.int32, (LANES, LANES), 1) // HEAD_DIM
    return jnp.where(r == c, 1.0, 0.0).astype(BF16)


def _head_sum(x, ones):
    parts = [_mm_exact_rhs(x[:, p * LANES:(p + 1) * LANES], ones) for p in range(x.shape[1] // LANES)]
    return jnp.concatenate(parts, axis=1)


def _sigmoid(x):
    return 1.0 / (1.0 + jnp.exp(-x))


def _rms(x, g):
    return x * lax.rsqrt(jnp.mean(x * x, axis=-1, keepdims=True) + RMS_EPS) * g


def _params(*sem):
    return pltpu.CompilerParams(dimension_semantics=sem, vmem_limit_bytes=VMEM_LIMIT)


def _full(shape):
    return pl.BlockSpec(shape, lambda *_: (0,) * len(shape))


def _rwkv_pre_body(x_ref, xp_ref, na_ref, mu_ref, wr_ref, wwd_ref, wk_ref, wv_ref, wad_ref, wg_ref,
                   w0_ref, w2_ref, a0_ref, a2_ref, kkw_ref, kaw_ref, rk_ref,
                   r_ref, ld_ref, k_ref, v_ref, kk_ref, nb_ref, bonus_ref, g_ref):
    x = x_ref[0]
    gn = na_ref[...]
    h = _rms(x, gn)
    hp = _rms(xp_ref[0], gn)
    prev = jnp.where(pl.program_id(1) == 0, 0.0, hp[7:8, :])
    rolled = pltpu.roll(h, 1, axis=0)
    first = lax.broadcasted_iota(jnp.int32, h.shape, 0) == 0
    xx = jnp.where(first, prev, rolled) - h

    def mix(i):
        return (h + xx * mu_ref[i:i + 1, :]).astype(BF16)

    r = _mm(mix(0), wr_ref[...])
    wdn = _mm(mix(1), wwd_ref[...])
    k = _mm(mix(2), wk_ref[...])
    v = _mm(mix(3), wv_ref[...])
    adn = _mm(mix(4), wad_ref[...])
    gp = _mm(mix(5), wg_ref[...])

    z = -(w0_ref[...] + _mm(jnp.tanh(wdn).astype(BF16), w2_ref[...]))
    softplus = jnp.maximum(z, 0.0) + jnp.log(1.0 + jnp.exp(-jnp.abs(z)))
    ld_ref[0] = -jnp.exp(-softplus - 0.5)
    a = _sigmoid(a0_ref[...] + _mm(adn.astype(BF16), a2_ref[...]))

    ones = _head_ones()
    kk = k * kkw_ref[...]
    kk = kk / jnp.maximum(jnp.sqrt(_head_sum(kk * kk, ones)), 1e-12)
    k2 = k * (1.0 + (a - 1.0) * kaw_ref[...])
    r_ref[0] = r
    k_ref[0] = k2
    v_ref[0] = v
    kk_ref[0] = kk
    nb_ref[0] = -(kk * a)
    bonus_ref[0] = _head_sum(r * k2 * rk_ref[...], ones) * v
    g_ref[0] = gp * _sigmoid(gp)


def _rwkv_pre(x, norm_a, mu, w_in, w0, w2, a0, a2, kkw, kaw, rk, tm=256):
    B, T, D = x.shape
    offs = [0, D, D + LORA, 2 * D + LORA, 3 * D + LORA, 3 * D + 2 * LORA, 4 * D + 2 * LORA]
    ws = [w_in[:, offs[i]:offs[i + 1]].astype(BF16) for i in range(6)]
    row = lambda a: a.reshape(1, D)
    big = pl.BlockSpec((1, tm, D), lambda b, i: (b, i, 0))
    prev = pl.BlockSpec((1, 8, D), lambda b, i: (b, jnp.maximum(i * (tm // 8) - 1, 0), 0))
    ins = [x, x, row(norm_a), mu] + ws + [row(w0), w2.astype(BF16), row(a0), a2.astype(BF16),
                                          row(kkw), row(kaw), row(rk)]
    in_specs = [big, prev] + [_full(a.shape) for a in ins[2:]]
    out = jax.ShapeDtypeStruct((B, T, D), F32)
    return pl.pallas_call(
        _rwkv_pre_body, grid=(B, T // tm), in_specs=in_specs, out_specs=[big] * 8, out_shape=[out] * 8,
        compiler_params=_params("parallel", "parallel"), name="rwkv_pre")(*ins)


def _stack(x, lane_lo):
    return jnp.concatenate([jnp.where(lane_lo, x, 0.0), jnp.where(lane_lo, 0.0, x)], axis=0)


def _unstack(z):
    half = z.shape[0] // 2
    return z[:half] + z[half:]


def _rwkv_chunk_body(r_ref, ld_ref, k_ref, v_ref, kk_ref, nb_ref, qh_ref, yi_ref, g_ref, c_ref, *, nc):
    L = CHUNK
    S = 2 * L
    lane_lo = lax.broadcasted_iota(jnp.int32, (L, LANES), 1) < HEAD_DIM
    ti = lax.broadcasted_iota(jnp.int32, (L, L), 0)
    tj = lax.broadcasted_iota(jnp.int32, (L, L), 1)
    tri = jnp.where(tj <= ti, 1.0, 0.0).astype(BF16)
    pi = lax.broadcasted_iota(jnp.int32, (2 * S, 2 * S), 0)
    pj = lax.broadcasted_iota(jnp.int32, (2 * S, 2 * S), 1)
    same_head = ((pi >> 6) & 1) == ((pj >> 6) & 1)
    t_i, s_j = pi & (L - 1), pj & (L - 1)
    valid = same_head & ((s_j < t_i) | ((pi >= S) & (s_j == t_i)))
    ei = lax.broadcasted_iota(jnp.int32, (S, S), 0)
    ej = lax.broadcasted_iota(jnp.int32, (S, S), 1)
    eye = ei == ej
    stack = lambda x: _bf(_stack(x, lane_lo))

    cs = range(nc)
    each = lambda f, *xs: [f(*a) for a in zip(*xs)]
    load = lambda ref: [ref[0, c * L:(c + 1) * L, :] for c in cs]
    r, ld, k, v, kk, nb = (load(ref) for ref in (r_ref, ld_ref, k_ref, v_ref, kk_ref, nb_ref))

    def cumsum(x):
        x1, x2 = _split(x)
        x3 = (x - x1.astype(F32) - x2.astype(F32)).astype(BF16)
        return _mm(tri, x1) + (_mm(tri, x2) + _mm(tri, x3))

    cum = each(cumsum, ld)
    cref = [x[L // 2 - 1:L // 2, :] for x in cum]
    clast = [x[L - 1:L, :] for x in cum]
    ecum = each(lambda a, b: a - b, cum, cref)
    e_in = [jnp.exp(x) for x in ecum]
    e_ex = each(lambda a, b: jnp.exp(a - b), ecum, ld)
    e_neg = [jnp.exp(-x) for x in ecum]
    p_ref = [jnp.exp(x) for x in cref]
    e_last = each(lambda a, b: jnp.exp(a - b), clast, cref)
    mul = lambda a, b: a * b
    q_m, q_n = each(mul, kk, e_ex), each(mul, r, e_in)
    k_b, k_k = each(mul, nb, e_neg), each(mul, k, e_neg)
    lhs = each(lambda a, b: jnp.concatenate([stack(a), stack(b)], axis=0), q_m, q_n)
    rhs = each(lambda a, b: jnp.concatenate([stack(a), stack(b)], axis=0), k_b, k_k)
    pm = each(lambda a, b: jnp.where(valid, _mm(a, b, NT), 0.0), lhs, rhs)
    m_b = [x[:S, :S] for x in pm]
    m_k, n_b, n_k = ([_bf(x[rs, cs_]) for x in pm] for rs, cs_ in
                     ((slice(0, S), slice(S, 2 * S)), (slice(S, 2 * S), slice(0, S)), (slice(S, 2 * S), slice(S, 2 * S))))
    tm = [jnp.where(eye, 1.0, 0.0) + x for x in m_b]
    xp = m_b
    for _ in range(5):
        xp = [_mm(xb, xb) for xb in (_bf(x) for x in xp)]
        tm = each(lambda t, x: t + _mm(_bf(t), _bf(x)), tm, xp)
    sv = [stack(x) for x in v]
    mkv = each(lambda a, b: _bf(_mm(a, b)), m_k, sv)
    qm0 = each(lambda a, b: stack(a * b), q_m, p_ref)
    wu = each(lambda t, a, b: _bf(_mm(_bf(t), jnp.concatenate([a, b], axis=1))), tm, qm0, mkv)
    kbl_t = each(lambda a, b: _bf(_stack(a * b, lane_lo).T), k_b, e_last)
    kkl_t = each(lambda a, b: _bf(_stack(a * b, lane_lo).T), k_k, e_last)
    gc = each(_mm, kbl_t, wu)
    kv = each(_mm, kkl_t, sv)
    nw = each(_mm, n_b, wu)
    nv = each(_mm, n_k, sv)
    for c in cs:
        sl = slice(c * L, (c + 1) * L)
        qh_ref[0, sl, :] = r[c] * (e_in[c] * p_ref[c]) + _unstack(nw[c][:, :S])
        yi_ref[0, sl, :] = _unstack(nw[c][:, S:] + nv[c])
        g_ref[0, 0, c] = _unstack(gc[c][:, :S] + jnp.where(eye, jnp.exp(clast[c]), 0.0))
        c_ref[0, 0, c] = _unstack(gc[c][:, S:] + kv[c])


def _rwkv_chunks(r, ld, k, v, kk, nb, nc=4):
    B, T, D = r.shape
    L = CHUNK
    tt = nc * L
    nch = T // L
    nat = pl.BlockSpec((1, tt, LANES), lambda b, p, i: (b, i, p))
    op = pl.BlockSpec((1, 1, nc, L, LANES), lambda b, p, i: (b, p, i, 0, 0))
    nat_shape = jax.ShapeDtypeStruct((B, T, D), F32)
    op_shape = jax.ShapeDtypeStruct((B, N_PAIRS, nch, L, LANES), F32)
    return pl.pallas_call(
        functools.partial(_rwkv_chunk_body, nc=nc), grid=(B, N_PAIRS, T // tt),
        in_specs=[nat] * 6, out_specs=[nat, nat, op, op], out_shape=[nat_shape, nat_shape, op_shape, op_shape],
        compiler_params=_params("parallel", "parallel", "parallel"), name="rwkv_chunks")(r, ld, k, v, kk, nb)


def _rwkv_state_body(qh_ref, yi_ref, g_ref, c_ref, y_ref, h_ref, *, nc, npair):
    L = CHUNK
    lane_lo = lax.broadcasted_iota(jnp.int32, (L, LANES), 1) < HEAD_DIM

    @pl.when(pl.program_id(2) == 0)
    def _():
        h_ref[...] = jnp.zeros_like(h_ref)

    def chunk(c, hs):
        sl = pl.ds(pl.multiple_of(c * L, L), L)
        out = []
        for p in range(npair):
            lanes = slice(p * LANES, (p + 1) * LANES)
            lhs = jnp.concatenate([qh_ref[0, sl, lanes], _stack(g_ref[0, p, c], lane_lo)], axis=0)
            prod = _mm(_bf(lhs), _bf(hs[p]))
            y_ref[0, sl, lanes] = prod[:L] + yi_ref[0, sl, lanes]
            out.append(prod[L:] + _stack(c_ref[0, p, c], lane_lo))
        return tuple(out)

    hs = lax.fori_loop(0, nc, chunk, tuple(h_ref[p] for p in range(npair)))
    for p in range(npair):
        h_ref[p] = hs[p]


def _rwkv_state(qh, yi, g, c, nc=8, npair=4):
    B, T, D = qh.shape
    L = CHUNK
    tt = nc * L
    nat = pl.BlockSpec((1, tt, npair * LANES), lambda b, p, i: (b, i, p))
    op = pl.BlockSpec((1, npair, nc, L, LANES), lambda b, p, i: (b, p, i, 0, 0))
    return pl.pallas_call(
        functools.partial(_rwkv_state_body, nc=nc, npair=npair), grid=(B, N_PAIRS // npair, T // tt),
        in_specs=[nat, nat, op, op], out_specs=nat, out_shape=jax.ShapeDtypeStruct((B, T, D), F32),
        scratch_shapes=[pltpu.VMEM((npair, 2 * HEAD_DIM, LANES), F32)],
        compiler_params=_params("parallel", "parallel", "arbitrary"), name="rwkv_state")(qh, yi, g, c)


def _rwkv_post_body(x_ref, y_ref, bonus_ref, g_ref, gw_ref, gb_ref, wo_ref, o_ref):
    ones = _head_ones()
    y = y_ref[0]
    mean = _head_sum(y, ones) * (1.0 / HEAD_DIM)
    d = y - mean
    var = _head_sum(d * d, ones) * (1.0 / HEAD_DIM)
    yn = d * lax.rsqrt(var + GN_EPS) * gw_ref[...] + gb_ref[...]
    z = ((yn + bonus_ref[0]) * g_ref[0]).astype(BF16)
    o_ref[0] = x_ref[0] + _mm(z, wo_ref[...])


def _rwkv_post(x, y, bonus, g, gn_w, gn_b, w_out, tm=512):
    B, T, D = x.shape
    big = pl.BlockSpec((1, tm, D), lambda b, i: (b, i, 0))
    ins = [x, y, bonus, g, gn_w.reshape(1, D), gn_b.reshape(1, D), w_out.astype(BF16)]
    return pl.pallas_call(
        _rwkv_post_body, grid=(B, T // tm), in_specs=[big] * 4 + [_full(a.shape) for a in ins[4:]],
        out_specs=big, out_shape=jax.ShapeDtypeStruct((B, T, D), F32),
        compiler_params=_params("parallel", "parallel"), name="rwkv_post")(*ins)


def _head_norm_rope_t(xt, gain, cos, sin):
    heads = []
    for h in range(N_HEADS):
        xh = xt[h * HEAD_DIM:(h + 1) * HEAD_DIM, :]
        xh = xh * lax.rsqrt(jnp.mean(xh * xh, axis=0, keepdims=True) + RMS_EPS) * gain
        x1, x2 = xh[:ROPE_HALF], xh[ROPE_HALF:ROPE_DIM]
        heads += [x1 * cos - x2 * sin, x2 * cos + x1 * sin, xh[ROPE_DIM:]]
    return jnp.concatenate(heads, axis=0)


def _moba_pre_body(x_ref, nkv_ref, nb_ref, wkv_ref, wq_ref, wg_ref, kn_ref, qn_ref, cos_ref, sin_ref,
                   k_ref, km_ref, qt_ref, vt_ref, g_ref):
    x = x_ref[0]
    xn = x * lax.rsqrt(jnp.mean(x * x, axis=-1, keepdims=True) + RMS_EPS)
    h_kv = (xn * nkv_ref[...]).astype(BF16)
    h_b = (xn * nb_ref[...]).astype(BF16)
    cos, sin = cos_ref[...], sin_ref[...]
    tile = lambda g: jnp.concatenate([g] * (x.shape[0] // LANES), axis=1)
    kvt = _mm(wkv_ref[...], h_kv, NT)
    vt_ref[0, 0] = kvt[D_MODEL:].astype(BF16)
    kt = _head_norm_rope_t(kvt[:D_MODEL], tile(kn_ref[...]), cos, sin)
    tag = lax.broadcasted_iota(jnp.int32, (TAG_LANES, x.shape[0]), 0) == pl.program_id(1)
    tag = jnp.where(tag, 1.0, 0.0)
    for h in range(N_HEADS):
        k_aug = jnp.concatenate([kt[h * HEAD_DIM:(h + 1) * HEAD_DIM], tag], axis=0).T
        k_ref[0, h] = k_aug.astype(BF16)
        km_ref[0, 0, h:h + 1, :] = jnp.mean(k_aug, axis=0, keepdims=True)
    qt = _head_norm_rope_t(_mm(wq_ref[...], h_b, NT), tile(qn_ref[...]), cos, sin)
    qt_ref[0, 0] = (qt * (LOG2E / HEAD_DIM ** 0.5)).astype(BF16)
    gate = _mm(h_b, wg_ref[...])
    g_ref[0] = gate * _sigmoid(gate)


def _rope_tables_t(T):
    inv_freq = jnp.power(jnp.float32(ROPE_THETA), -jnp.arange(0, ROPE_DIM, 2, dtype=F32) / ROPE_DIM)
    ang = inv_freq[:, None] * jnp.arange(T, dtype=F32)[None, :]
    return jnp.cos(ang), jnp.sin(ang)


def _moba_pre(x, norm_kv, norm_b, w_kv, w_in, k_norm, q_norm):
    B, T, D = x.shape
    tm = MOBA_BLOCK
    nb = T // tm
    cos, sin = _rope_tables_t(T)
    bcast = lambda g: jnp.broadcast_to(g.reshape(HEAD_DIM, 1), (HEAD_DIM, LANES))
    ins = [x, norm_kv.reshape(1, D), norm_b.reshape(1, D), w_kv.T.astype(BF16), w_in[:, :D].T.astype(BF16),
           w_in[:, D:].astype(BF16), bcast(k_norm), bcast(q_norm), cos, sin]
    big = pl.BlockSpec((1, tm, D), lambda b, i: (b, i, 0))
    tab = pl.BlockSpec((ROPE_HALF, tm), lambda b, i: (0, i))
    tr = pl.BlockSpec((1, 1, D, tm), lambda b, i: (b, i, 0, 0))
    in_specs = [big] + [_full(a.shape) for a in ins[1:8]] + [tab, tab]
    assert tm == MOBA_BLOCK and nb <= TAG_LANES
    out_specs = [pl.BlockSpec((1, N_HEADS, tm, LANES), lambda b, i: (b, 0, i, 0)),
                 pl.BlockSpec((1, 1, N_HEADS, LANES), lambda b, i: (b, i, 0, 0)), tr, tr, big]
    out_shape = [jax.ShapeDtypeStruct((B, N_HEADS, T, LANES), BF16), jax.ShapeDtypeStruct((B, nb, N_HEADS, LANES), F32),
                 jax.ShapeDtypeStruct((B, nb, D, tm), BF16), jax.ShapeDtypeStruct((B, nb, D, tm), BF16),
                 jax.ShapeDtypeStruct((B, T, D), F32)]
    return pl.pallas_call(
        _moba_pre_body, grid=(B, nb), in_specs=in_specs, out_specs=out_specs, out_shape=out_shape,
        compiler_params=_params("parallel", "parallel"), name="moba_pre")(*ins)


def _moba_attn_body(k_ref, km_ref, qt_ref, vt_ref, o_ref, sa_ref, sb_ref, *, nb):
    BS = MOBA_BLOCK
    i = pl.program_id(2)
    qt = qt_ref[0, 0]
    blk_f = lax.broadcasted_iota(jnp.int32, (nb, BS), 0).astype(F32)
    key_pos = lax.broadcasted_iota(jnp.int32, (BS, BS), 0)
    qry_pos = lax.broadcasted_iota(jnp.int32, (BS, BS), 1)
    ones = jnp.ones((SUM_ROWS, BS), BF16)
    v_aug = lambda h, j: jnp.concatenate([vt_ref[0, j, h * HEAD_DIM:(h + 1) * HEAD_DIM, :], ones], axis=0)
    own = pl.ds(pl.multiple_of(i * BS, BS), BS)
    init, q_aug = [], []
    for h in range(2):
        q = qt[h * HEAD_DIM:(h + 1) * HEAD_DIM]
        q_plain = jnp.concatenate([q, jnp.zeros((TAG_LANES, BS), BF16)], axis=0)
        km_hi, km_lo = _split(km_ref[0, h])
        gs = _mm(km_hi, q_plain) + _mm(km_lo, q_plain)
        work = jnp.where(blk_f < i.astype(F32), gs, -jnp.inf)
        f = jnp.full((nb, BS), NEG, F32)
        for _ in range(MOBA_TOPK):
            mx = jnp.max(work, axis=0, keepdims=True)
            first = jnp.min(jnp.where(work == mx, blk_f, float(nb)), axis=0, keepdims=True)
            hit = blk_f == first
            f = jnp.where(hit, jnp.where(mx > -jnp.inf, 0.0, f), f)
            work = jnp.where(hit, -jnp.inf, work)
        f_rows = jnp.concatenate([f, jnp.full((TAG_LANES - nb, BS), NEG, F32)], axis=0) if nb < TAG_LANES else f
        q_aug.append(jnp.concatenate([q, _bf(f_rows)], axis=0))
        s = jnp.where(key_pos <= qry_pos, _mm(k_ref[0, h, own, :], q_plain), NEG)
        m0 = jnp.max(s, axis=0, keepdims=True)
        init.append((m0, _mm(v_aug(h, i), _bf(jnp.exp2(s - m0)))))

    hu = [(h, u) for h in range(2) for u in range(2)]
    block = lambda jj, u: jnp.minimum(2 * jj + u, nb - 1)

    def scores(jj, dst_ref):
        for c, (h, u) in enumerate(hu):
            rows = pl.ds(pl.multiple_of(block(jj, u) * BS, BS), BS)
            dst_ref[c] = _mm(k_ref[0, h, rows, :], q_aug[h])

    def consume(src_ref, jj, carry):
        s = [src_ref[c] for c in range(4)]
        smax = [jnp.max(x, axis=0, keepdims=True) for x in s]
        m_new = [jnp.maximum(carry[h][0], jnp.maximum(smax[2 * h], smax[2 * h + 1])) for h in range(2)]
        pv = [_mm(v_aug(h, block(jj, u)), _bf(jnp.exp2(x - m_new[h]))) for x, (h, u) in zip(s, hu)]
        return tuple((m_new[h], jnp.exp2(carry[h][0] - m_new[h]) * carry[h][1] + (pv[2 * h] + pv[2 * h + 1]))
                     for h in range(2))

    scores(0, sa_ref)

    def two_steps(t, carry):
        scores(2 * t + 1, sb_ref)
        carry = consume(sa_ref, 2 * t, carry)
        scores(2 * t + 2, sa_ref)
        return consume(sb_ref, 2 * t + 1, carry)

    fin = lax.fori_loop(0, ((i + 1) // 2 + 1) // 2, two_steps, tuple(init))
    o_ref[0] = jnp.concatenate([acc[:HEAD_DIM] / acc[HEAD_DIM:HEAD_DIM + 1] for _, acc in fin], axis=0).T


def _moba_attn(k, kmean, qt, vt):
    B, H, T, _ = k.shape
    BS = MOBA_BLOCK
    nb = T // BS
    in_specs = [pl.BlockSpec((1, 2, T, LANES), lambda b, p, i: (b, p, 0, 0)),
                pl.BlockSpec((1, 2, nb, LANES), lambda b, p, i: (b, p, 0, 0)),
                pl.BlockSpec((1, 1, LANES, BS), lambda b, p, i: (b, i, p, 0)),
                pl.BlockSpec((1, nb, LANES, BS), lambda b, p, i: (b, 0, p, 0))]
    return pl.pallas_call(
        functools.partial(_moba_attn_body, nb=nb), grid=(B, N_PAIRS, nb), in_specs=in_specs,
        out_specs=pl.BlockSpec((1, BS, LANES), lambda b, p, i: (b, i, p)),
        out_shape=jax.ShapeDtypeStruct((B, T, H * HEAD_DIM), F32),
        scratch_shapes=[pltpu.VMEM((4, BS, BS), F32), pltpu.VMEM((4, BS, BS), F32)],
        compiler_params=_params("parallel", "parallel", "arbitrary"), name="moba_attn")(k, kmean, qt, vt)


def _moba_post_body(x_ref, o_ref, g_ref, wo_ref, out_ref):
    out_ref[0] = x_ref[0] + _mm((o_ref[0] * g_ref[0]).astype(BF16), wo_ref[...])


def _moba_post(x, o, g, w_out, tm=512):
    B, T, D = x.shape
    big = pl.BlockSpec((1, tm, D), lambda b, i: (b, i, 0))
    return pl.pallas_call(
        _moba_post_body, grid=(B, T // tm), in_specs=[big, big, big, _full((D, D))], out_specs=big,
        out_shape=jax.ShapeDtypeStruct((B, T, D), F32),
        compiler_params=_params("parallel", "parallel"), name="moba_post")(x, o, g, w_out.astype(BF16))


def _rwkv_layer(x, norm_a, mu, w_in, w0, w2, a0, a2, kkw, kaw, rk, gn_w, gn_b, w_out):
    r, ld, k, v, kk, nb, bonus, g = _rwkv_pre(x, norm_a, mu, w_in, w0, w2, a0, a2, kkw, kaw, rk.reshape(-1))
    qh, yi, gm, cm = _rwkv_chunks(r, ld, k, v, kk, nb)
    y = _rwkv_state(qh, yi, gm, cm)
    return _rwkv_post(x, y, bonus, g, gn_w, gn_b, w_out)


def _moba_layer(x, norm_kv, w_kv, k_norm, norm_b, w_in, q_norm, w_out):
    B, T, D = x.shape
    k, kmean, qt, vt, g = _moba_pre(x, norm_kv, norm_b, w_kv, w_in, k_norm, q_norm)
    o = _moba_attn(k, kmean.transpose(0, 2, 1, 3), qt, vt)
    return _moba_post(x, o, g, w_out)


def kernel(x, norm_a, mu_a, w_in_a, w0_a, w2_a, a0_a, a2_a, kk_a, ka_a, rk_a, gn_w_a, gn_b_a, w_out_a, norm_kv, w_kv, k_norm, norm_b, w_in_b, q_norm_b, w_out_b):
    x = _rwkv_layer(x, norm_a[0], mu_a[0], w_in_a[0], w0_a[0], w2_a[0], a0_a[0], a2_a[0], kk_a[0], ka_a[0],
                    rk_a[0], gn_w_a[0], gn_b_a[0], w_out_a[0])
    return _moba_layer(x, norm_kv, w_kv, k_norm, norm_b[0], w_in_b[0], q_norm_b[0], w_out_b[0])
```

```python
import functools

import jax
import jax.numpy as jnp
from jax import lax
from jax.experimental import pallas as pl
from jax.experimental.pallas import tpu as pltpu

F32 = jnp.float32
BF16 = jnp.bfloat16

D_MODEL = 1024
HEAD_DIM = 64
N_HEADS = D_MODEL // HEAD_DIM
LANES = 128
N_PAIRS = D_MODEL // LANES
LORA = 64
MOBA_BLOCK = 256
MOBA_TOPK = 3
ROPE_THETA = 500000.0
ROPE_DIM = HEAD_DIM // 4
ROPE_HALF = ROPE_DIM // 2
RMS_EPS = 1e-6
GN_EPS = 64e-5
NEG = -1e30
LOG2E = 1.4426950408889634
SUM_ROWS = 16
TAG_LANES = LANES - HEAD_DIM
CHUNK = 64
VMEM_LIMIT = 56 * 1024 * 1024

NN = (((1,), (0,)), ((), ()))
NT = (((1,), (1,)), ((), ()))
TN = (((0,), (0,)), ((), ()))


def _mm(a, b, dims=NN):
    return lax.dot_general(a, b, dims, preferred_element_type=F32)


def _split(x):
    hi = x.astype(BF16)
    lo = (x - hi.astype(F32)).astype(BF16)
    return hi, lo


def _bf(x):
    return x.astype(BF16)


def _head_ones():
    r = lax.broadcasted_iota(jnp.int32, (LANES, LANES), 0) // HEAD_DIM
    c = lax.broadcasted_iota(jnp.int32, (LANES, LANES), 1) // HEAD_DIM
    return jnp.where(r == c, 1.0, 0.0).astype(BF16)


def _head_sum(x, ones):
    parts = [_mm(_bf(x[:, p * LANES:(p + 1) * LANES]), ones) for p in range(x.shape[1] // LANES)]
    return jnp.concatenate(parts, axis=1)


def _sigmoid(x):
    return 1.0 / (1.0 + jnp.exp(-x))


def _rms(x, g):
    return x * lax.rsqrt(jnp.mean(x * x, axis=-1, keepdims=True) + RMS_EPS) * g


def _params(*sem):
    return pltpu.CompilerParams(dimension_semantics=sem, vmem_limit_bytes=VMEM_LIMIT)


def _full(shape):
    return pl.BlockSpec(shape, lambda *_: (0,) * len(shape))


def _rwkv_pre_body(x_ref, xp_ref, na_ref, mu_ref, wr_ref, wwd_ref, wk_ref, wv_ref, wad_ref, wg_ref,
                   w0_ref, w2_ref, a0_ref, a2_ref, kkw_ref, kaw_ref, rk_ref,
                   r_ref, ld_ref, k_ref, v_ref, kk_ref, nb_ref, bonus_ref, g_ref):
    x = x_ref[0]
    gn = na_ref[...]
    h = _rms(x, gn)
    hp = _rms(xp_ref[0], gn)
    prev = jnp.where(pl.program_id(1) == 0, 0.0, hp[7:8, :])
    rolled = pltpu.roll(h, 1, axis=0)
    first = lax.broadcasted_iota(jnp.int32, h.shape, 0) == 0
    xx = jnp.where(first, prev, rolled) - h

    def mix(i):
        return (h + xx * mu_ref[i:i + 1, :]).astype(BF16)

    r = _mm(mix(0), wr_ref[...])
    wdn = _mm(mix(1), wwd_ref[...])
    k = _mm(mix(2), wk_ref[...])
    v = _mm(mix(3), wv_ref[...])
    adn = _mm(mix(4), wad_ref[...])
    gp = _mm(mix(5), wg_ref[...])

    z = -(w0_ref[...] + _mm(jnp.tanh(wdn).astype(BF16), w2_ref[...]))
    softplus = jnp.maximum(z, 0.0) + jnp.log(1.0 + jnp.exp(-jnp.abs(z)))
    ld_ref[0] = -jnp.exp(-softplus - 0.5)
    a = _sigmoid(a0_ref[...] + _mm(adn.astype(BF16), a2_ref[...]))

    ones = _head_ones()
    kk = k * kkw_ref[...]
    kk = kk / jnp.maximum(jnp.sqrt(_head_sum(kk * kk, ones)), 1e-12)
    k2 = k * (1.0 + (a - 1.0) * kaw_ref[...])
    r_ref[0] = r
    k_ref[0] = k2
    v_ref[0] = v
    kk_ref[0] = kk
    nb_ref[0] = -(kk * a)
    bonus_ref[0] = _bf(_head_sum(r * k2 * rk_ref[...], ones) * v)
    g_ref[0] = _bf(gp * _sigmoid(gp))


def _rwkv_pre(x, norm_a, mu, w_in, w0, w2, a0, a2, kkw, kaw, rk, tm=256):
    B, T, D = x.shape
    offs = [0, D, D + LORA, 2 * D + LORA, 3 * D + LORA, 3 * D + 2 * LORA, 4 * D + 2 * LORA]
    ws = [w_in[:, offs[i]:offs[i + 1]].astype(BF16) for i in range(6)]
    row = lambda a: a.reshape(1, D)
    big = pl.BlockSpec((1, tm, D), lambda b, i: (b, i, 0))
    prev = pl.BlockSpec((1, 8, D), lambda b, i: (b, jnp.maximum(i * (tm // 8) - 1, 0), 0))
    ins = [x, x, row(norm_a), mu] + ws + [row(w0), w2.astype(BF16), row(a0), a2.astype(BF16),
                                          row(kkw), row(kaw), row(rk)]
    in_specs = [big, prev] + [_full(a.shape) for a in ins[2:]]
    out = lambda dt: jax.ShapeDtypeStruct((B, T, D), dt)
    return pl.pallas_call(
        _rwkv_pre_body, grid=(B, T // tm), in_specs=in_specs, out_specs=[big] * 8,
        out_shape=[out(F32)] * 6 + [out(BF16)] * 2,
        compiler_params=_params("parallel", "parallel"), name="rwkv_pre")(*ins)


def _stack(x, lane_lo):
    return jnp.concatenate([jnp.where(lane_lo, x, 0.0), jnp.where(lane_lo, 0.0, x)], axis=0)


def _unstack(z):
    half = z.shape[0] // 2
    return z[:half] + z[half:]


def _rwkv_chunk_body(r_ref, ld_ref, k_ref, v_ref, kk_ref, nb_ref, qh_ref, yi_ref, g_ref, c_ref, *, nc):
    L = CHUNK
    S = 2 * L
    lane_lo = lax.broadcasted_iota(jnp.int32, (L, LANES), 1) < HEAD_DIM
    ti = lax.broadcasted_iota(jnp.int32, (L, L), 0)
    tj = lax.broadcasted_iota(jnp.int32, (L, L), 1)
    tri = jnp.where(tj <= ti, 1.0, 0.0).astype(BF16)
    pi = lax.broadcasted_iota(jnp.int32, (2 * S, 2 * S), 0)
    pj = lax.broadcasted_iota(jnp.int32, (2 * S, 2 * S), 1)
    same_head = ((pi >> 6) & 1) == ((pj >> 6) & 1)
    t_i, s_j = pi & (L - 1), pj & (L - 1)
    valid = same_head & ((s_j < t_i) | ((pi >= S) & (s_j == t_i)))
    ei = lax.broadcasted_iota(jnp.int32, (S, S), 0)
    ej = lax.broadcasted_iota(jnp.int32, (S, S), 1)
    eye = ei == ej
    stack = lambda x: _bf(_stack(x, lane_lo))

    cs = range(nc)
    each = lambda f, *xs: [f(*a) for a in zip(*xs)]
    load = lambda ref: [ref[0, c * L:(c + 1) * L, :] for c in cs]
    r, ld, k, v, kk, nb = (load(ref) for ref in (r_ref, ld_ref, k_ref, v_ref, kk_ref, nb_ref))

    def cumsum(x):
        x1, x2 = _split(x)
        x3 = (x - x1.astype(F32) - x2.astype(F32)).astype(BF16)
        return _mm(tri, x1) + (_mm(tri, x2) + _mm(tri, x3))

    cum = each(cumsum, ld)
    cref = [x[L // 2 - 1:L // 2, :] for x in cum]
    clast = [x[L - 1:L, :] for x in cum]
    ecum = each(lambda a, b: a - b, cum, cref)
    e_in = [jnp.exp(x) for x in ecum]
    e_ex = each(lambda a, b: jnp.exp(a - b), ecum, ld)
    e_neg = [jnp.exp(-x) for x in ecum]
    p_ref = [jnp.exp(x) for x in cref]
    e_last = each(lambda a, b: jnp.exp(a - b), clast, cref)
    mul = lambda a, b: a * b
    q_m, q_n = each(mul, kk, e_ex), each(mul, r, e_in)
    k_b, k_k = each(mul, nb, e_neg), each(mul, k, e_neg)
    lhs = each(lambda a, b: jnp.concatenate([stack(a), stack(b)], axis=0), q_m, q_n)
    rhs = each(lambda a, b: jnp.concatenate([stack(a), stack(b)], axis=0), k_b, k_k)
    pm = each(lambda a, b: jnp.where(valid, _mm(a, b, NT), 0.0), lhs, rhs)
    m_b = [x[:S, :S] for x in pm]
    m_k, n_b, n_k = ([_bf(x[rs, cs_]) for x in pm] for rs, cs_ in
                     ((slice(0, S), slice(S, 2 * S)), (slice(S, 2 * S), slice(0, S)), (slice(S, 2 * S), slice(S, 2 * S))))
    tm = [jnp.where(eye, 1.0, 0.0) + x for x in m_b]
    xp = m_b
    for _ in range(5):
        xp = [_mm(xb, xb) for xb in (_bf(x) for x in xp)]
        tm = each(lambda t, x: t + _mm(_bf(t), _bf(x)), tm, xp)
    sv = [stack(x) for x in v]
    mkv = each(lambda a, b: _bf(_mm(a, b)), m_k, sv)
    qm0 = each(lambda a, b: stack(a * b), q_m, p_ref)
    wu = each(lambda t, a, b: _bf(_mm(_bf(t), jnp.concatenate([a, b], axis=1))), tm, qm0, mkv)
    kbl_t = each(lambda a, b: _bf(_stack(a * b, lane_lo).T), k_b, e_last)
    kkl_t = each(lambda a, b: _bf(_stack(a * b, lane_lo).T), k_k, e_last)
    gc = each(_mm, kbl_t, wu)
    kv = each(_mm, kkl_t, sv)
    nw = each(_mm, n_b, wu)
    nv = each(_mm, n_k, sv)
    for c in cs:
        sl = slice(c * L, (c + 1) * L)
        qh_ref[0, sl, :] = r[c] * (e_in[c] * p_ref[c]) + _unstack(nw[c][:, :S])
        yi_ref[0, sl, :] = _unstack(nw[c][:, S:] + nv[c])
        g_ref[0, 0, c] = _unstack(gc[c][:, :S] + jnp.where(eye, jnp.exp(clast[c]), 0.0))
        c_ref[0, 0, c] = _unstack(gc[c][:, S:] + kv[c])


def _rwkv_chunks(r, ld, k, v, kk, nb, nc=16):
    B, T, D = r.shape
    L = CHUNK
    tt = nc * L
    nch = T // L
    nat = pl.BlockSpec((1, tt, LANES), lambda b, p, i: (b, i, p))
    op = pl.BlockSpec((1, 1, nc, L, LANES), lambda b, p, i: (b, p, i, 0, 0))
    nat_shape = jax.ShapeDtypeStruct((B, T, D), F32)
    op_shape = jax.ShapeDtypeStruct((B, N_PAIRS, nch, L, LANES), F32)
    return pl.pallas_call(
        functools.partial(_rwkv_chunk_body, nc=nc), grid=(B, N_PAIRS, T // tt),
        in_specs=[nat] * 6, out_specs=[nat, nat, op, op], out_shape=[nat_shape, nat_shape, op_shape, op_shape],
        compiler_params=_params("parallel", "parallel", "parallel"), name="rwkv_chunks")(r, ld, k, v, kk, nb)


def _rwkv_out_body(qh_ref, yi_ref, g_ref, c_ref, x_ref, bonus_ref, gate_ref, gw_ref, gb_ref, wo_ref,
                   o_ref, h_ref, y_ref, *, nc):
    L = CHUNK
    lane_lo = lax.broadcasted_iota(jnp.int32, (L, LANES), 1) < HEAD_DIM

    @pl.when(pl.program_id(1) == 0)
    def _():
        h_ref[...] = jnp.zeros_like(h_ref)

    def chunk(c, hs):
        sl = pl.ds(pl.multiple_of(c * L, L), L)
        out = []
        for p in range(N_PAIRS):
            lanes = slice(p * LANES, (p + 1) * LANES)
            lhs = jnp.concatenate([qh_ref[0, sl, lanes], _stack(g_ref[0, p, c], lane_lo)], axis=0)
            prod = _mm(_bf(lhs), _bf(hs[p]))
            y_ref[sl, lanes] = prod[:L] + yi_ref[0, sl, lanes]
            out.append(prod[L:] + _stack(c_ref[0, p, c], lane_lo))
        return tuple(out)

    hs = lax.fori_loop(0, nc, chunk, tuple(h_ref[p] for p in range(N_PAIRS)))
    for p in range(N_PAIRS):
        h_ref[p] = hs[p]

    ones = _head_ones()
    y = y_ref[...]
    mean = _head_sum(y, ones) * (1.0 / HEAD_DIM)
    d = y - mean
    var = _head_sum(d * d, ones) * (1.0 / HEAD_DIM)
    yn = d * lax.rsqrt(var + GN_EPS) * gw_ref[...] + gb_ref[...]
    z = _bf((yn + bonus_ref[0].astype(F32)) * gate_ref[0].astype(F32))
    o_ref[0] = x_ref[0] + _mm(z, wo_ref[...])


def _rwkv_out(qh, yi, g, c, x, bonus, gate, gn_w, gn_b, w_out, nc=8):
    B, T, D = qh.shape
    L = CHUNK
    tt = nc * L
    big = pl.BlockSpec((1, tt, D), lambda b, i: (b, i, 0))
    op = pl.BlockSpec((1, N_PAIRS, nc, L, LANES), lambda b, i: (b, 0, i, 0, 0))
    ins = [qh, yi, g, c, x, bonus, gate, gn_w.reshape(1, D), gn_b.reshape(1, D), w_out.astype(BF16)]
    return pl.pallas_call(
        functools.partial(_rwkv_out_body, nc=nc), grid=(B, T // tt),
        in_specs=[big, big, op, op, big, big, big] + [_full(a.shape) for a in ins[7:]],
        out_specs=big, out_shape=jax.ShapeDtypeStruct((B, T, D), F32),
        scratch_shapes=[pltpu.VMEM((N_PAIRS, 2 * HEAD_DIM, LANES), F32), pltpu.VMEM((tt, D), F32)],
        compiler_params=_params("parallel", "arbitrary"), name="rwkv_out")(*ins)


def _head_norm_rope_t(xt, gain, cos, sin):
    heads = []
    for h in range(N_HEADS):
        xh = xt[h * HEAD_DIM:(h + 1) * HEAD_DIM, :]
        xh = xh * lax.rsqrt(jnp.mean(xh * xh, axis=0, keepdims=True) + RMS_EPS) * gain
        x1, x2 = xh[:ROPE_HALF], xh[ROPE_HALF:ROPE_DIM]
        heads += [x1 * cos - x2 * sin, x2 * cos + x1 * sin, xh[ROPE_DIM:]]
    return jnp.concatenate(heads, axis=0)


def _moba_pre_body(x_ref, nkv_ref, nb_ref, wkv_ref, wq_ref, wg_ref, kn_ref, qn_ref, cos_ref, sin_ref,
                   k_ref, km_ref, qt_ref, vt_ref, g_ref):
    x = x_ref[0]
    xn = x * lax.rsqrt(jnp.mean(x * x, axis=-1, keepdims=True) + RMS_EPS)
    h_kv = (xn * nkv_ref[...]).astype(BF16)
    h_b = (xn * nb_ref[...]).astype(BF16)
    cos, sin = cos_ref[...], sin_ref[...]
    tile = lambda g: jnp.concatenate([g] * (x.shape[0] // LANES), axis=1)
    kvt = _mm(wkv_ref[...], h_kv, NT)
    vt_ref[0, 0] = kvt[D_MODEL:].astype(BF16)
    kt = _head_norm_rope_t(kvt[:D_MODEL], tile(kn_ref[...]), cos, sin)
    tag = lax.broadcasted_iota(jnp.int32, (TAG_LANES, x.shape[0]), 0) == pl.program_id(1)
    tag = jnp.where(tag, 1.0, 0.0)
    for h in range(N_HEADS):
        k_aug = jnp.concatenate([kt[h * HEAD_DIM:(h + 1) * HEAD_DIM], tag], axis=0).T
        k_ref[0, h] = k_aug.astype(BF16)
        km_ref[0, 0, h:h + 1, :] = jnp.mean(k_aug, axis=0, keepdims=True)
    qt = _head_norm_rope_t(_mm(wq_ref[...], h_b, NT), tile(qn_ref[...]), cos, sin)
    qt_ref[0, 0] = (qt * (LOG2E / HEAD_DIM ** 0.5)).astype(BF16)
    gate = _mm(wg_ref[...], h_b, NT)
    g_ref[0] = _bf(gate * _sigmoid(gate))


def _rope_tables_t(T):
    inv_freq = jnp.power(jnp.float32(ROPE_THETA), -jnp.arange(0, ROPE_DIM, 2, dtype=F32) / ROPE_DIM)
    ang = inv_freq[:, None] * jnp.arange(T, dtype=F32)[None, :]
    return jnp.cos(ang), jnp.sin(ang)


def _moba_pre(x, norm_kv, norm_b, w_kv, w_in, k_norm, q_norm):
    B, T, D = x.shape
    tm = MOBA_BLOCK
    nb = T // tm
    cos, sin = _rope_tables_t(T)
    bcast = lambda g: jnp.broadcast_to(g.reshape(HEAD_DIM, 1), (HEAD_DIM, LANES))
    ins = [x, norm_kv.reshape(1, D), norm_b.reshape(1, D), w_kv.T.astype(BF16), w_in[:, :D].T.astype(BF16),
           w_in[:, D:].T.astype(BF16), bcast(k_norm), bcast(q_norm), cos, sin]
    big = pl.BlockSpec((1, tm, D), lambda b, i: (b, i, 0))
    tab = pl.BlockSpec((ROPE_HALF, tm), lambda b, i: (0, i))
    tr = pl.BlockSpec((1, 1, D, tm), lambda b, i: (b, i, 0, 0))
    in_specs = [big] + [_full(a.shape) for a in ins[1:8]] + [tab, tab]
    assert tm == MOBA_BLOCK and nb <= TAG_LANES
    out_specs = [pl.BlockSpec((1, N_HEADS, tm, LANES), lambda b, i: (b, 0, i, 0)),
                 pl.BlockSpec((1, 1, N_HEADS, LANES), lambda b, i: (b, i, 0, 0)), tr, tr,
                 pl.BlockSpec((1, D, tm), lambda b, i: (b, 0, i))]
    out_shape = [jax.ShapeDtypeStruct((B, N_HEADS, T, LANES), BF16), jax.ShapeDtypeStruct((B, nb, N_HEADS, LANES), F32),
                 jax.ShapeDtypeStruct((B, nb, D, tm), BF16), jax.ShapeDtypeStruct((B, nb, D, tm), BF16),
                 jax.ShapeDtypeStruct((B, D, T), BF16)]
    return pl.pallas_call(
        _moba_pre_body, grid=(B, nb), in_specs=in_specs, out_specs=out_specs, out_shape=out_shape,
        compiler_params=_params("parallel", "parallel"), name="moba_pre")(*ins)


def _moba_attn_body(k_ref, km_ref, qt_ref, vt_ref, o_ref, sa_ref, sb_ref, *, nb):
    BS = MOBA_BLOCK
    i = pl.program_id(2)
    qt = qt_ref[0, 0]
    blk_f = lax.broadcasted_iota(jnp.int32, (nb, BS), 0).astype(F32)
    key_pos = lax.broadcasted_iota(jnp.int32, (BS, BS), 0)
    qry_pos = lax.broadcasted_iota(jnp.int32, (BS, BS), 1)
    ones = jnp.ones((SUM_ROWS, BS), BF16)
    v_aug = lambda h, j: jnp.concatenate([vt_ref[0, j, h * HEAD_DIM:(h + 1) * HEAD_DIM, :], ones], axis=0)
    own = pl.ds(pl.multiple_of(i * BS, BS), BS)
    heads = range(2)
    q = [qt[h * HEAD_DIM:(h + 1) * HEAD_DIM] for h in heads]
    q_plain = [jnp.concatenate([x, jnp.zeros((TAG_LANES, BS), BF16)], axis=0) for x in q]
    km = [_split(km_ref[0, h]) for h in heads]
    gs = [_mm(km[h][0], q_plain[h]) + _mm(km[h][1], q_plain[h]) for h in heads]
    s_own = [jnp.where(key_pos <= qry_pos, _mm(k_ref[0, h, own, :], q_plain[h]), NEG) for h in heads]
    work = [jnp.where(blk_f < i.astype(F32), x, -jnp.inf) for x in gs]
    f = [jnp.full((nb, BS), NEG, F32) for _ in heads]
    for _ in range(MOBA_TOPK):
        mx = [jnp.max(x, axis=0, keepdims=True) for x in work]
        first = [jnp.min(jnp.where(w == m, blk_f, float(nb)), axis=0, keepdims=True) for w, m in zip(work, mx)]
        hit = [blk_f == x for x in first]
        f = [jnp.where(h_, jnp.where(m > -jnp.inf, 0.0, f_), f_) for h_, m, f_ in zip(hit, mx, f)]
        work = [jnp.where(h_, -jnp.inf, w) for h_, w in zip(hit, work)]
    pad = [jnp.full((TAG_LANES - nb, BS), NEG, F32)] if nb < TAG_LANES else []
    q_aug = [jnp.concatenate([q[h], _bf(jnp.concatenate([f[h]] + pad, axis=0))], axis=0) for h in heads]
    hu = [(h, u) for h in range(2) for u in range(2)]
    block = lambda jj, u: jnp.minimum(2 * jj + u, nb - 1)

    def scores(jj, dst_ref):
        for c, (h, u) in enumerate(hu):
            rows = pl.ds(pl.multiple_of(block(jj, u) * BS, BS), BS)
            dst_ref[c] = _mm(k_ref[0, h, rows, :], q_aug[h])

    def consume(src_ref, jj, carry):
        s = [src_ref[c] for c in range(4)]
        smax = [jnp.max(x, axis=0, keepdims=True) for x in s]
        m_new = [jnp.maximum(carry[h][0], jnp.maximum(smax[2 * h], smax[2 * h + 1])) for h in range(2)]
        pv = [_mm(v_aug(h, block(jj, u)), _bf(jnp.exp2(x - m_new[h]))) for x, (h, u) in zip(s, hu)]
        return tuple((m_new[h], jnp.exp2(carry[h][0] - m_new[h]) * carry[h][1] + (pv[2 * h] + pv[2 * h + 1]))
                     for h in range(2))

    scores(0, sa_ref)
    m0 = [jnp.max(x, axis=0, keepdims=True) for x in s_own]
    init = [(m0[h], _mm(v_aug(h, i), _bf(jnp.exp2(s_own[h] - m0[h])))) for h in heads]

    def two_steps(t, carry):
        scores(2 * t + 1, sb_ref)
        carry = consume(sa_ref, 2 * t, carry)
        scores(2 * t + 2, sa_ref)
        return consume(sb_ref, 2 * t + 1, carry)

    fin = lax.fori_loop(0, ((i + 1) // 2 + 1) // 2, two_steps, tuple(init))
    o_ref[0] = _bf(jnp.concatenate([acc[:HEAD_DIM] / acc[HEAD_DIM:HEAD_DIM + 1] for _, acc in fin], axis=0))


def _moba_attn(k, kmean, qt, vt):
    B, H, T, _ = k.shape
    BS = MOBA_BLOCK
    nb = T // BS
    in_specs = [pl.BlockSpec((1, 2, T, LANES), lambda b, p, i: (b, p, 0, 0)),
                pl.BlockSpec((1, 2, nb, LANES), lambda b, p, i: (b, p, 0, 0)),
                pl.BlockSpec((1, 1, LANES, BS), lambda b, p, i: (b, i, p, 0)),
                pl.BlockSpec((1, nb, LANES, BS), lambda b, p, i: (b, 0, p, 0))]
    return pl.pallas_call(
        functools.partial(_moba_attn_body, nb=nb), grid=(B, N_PAIRS, nb), in_specs=in_specs,
        out_specs=pl.BlockSpec((1, LANES, BS), lambda b, p, i: (b, p, i)),
        out_shape=jax.ShapeDtypeStruct((B, H * HEAD_DIM, T), BF16),
        scratch_shapes=[pltpu.VMEM((4, BS, BS), F32), pltpu.VMEM((4, BS, BS), F32)],
        compiler_params=_params("parallel", "parallel", "arbitrary"), name="moba_attn")(k, kmean, qt, vt)


def _moba_post_body(x_ref, ot_ref, gt_ref, wo_ref, out_ref):
    zt = _bf(ot_ref[0].astype(F32) * gt_ref[0].astype(F32))
    out_ref[0] = x_ref[0] + _mm(zt, wo_ref[...], TN)


def _moba_post(x, ot, gt, w_out, tm=512):
    B, T, D = x.shape
    big = pl.BlockSpec((1, tm, D), lambda b, i: (b, i, 0))
    tr = pl.BlockSpec((1, D, tm), lambda b, i: (b, 0, i))
    return pl.pallas_call(
        _moba_post_body, grid=(B, T // tm), in_specs=[big, tr, tr, _full((D, D))], out_specs=big,
        out_shape=jax.ShapeDtypeStruct((B, T, D), F32),
        compiler_params=_params("parallel", "parallel"), name="moba_post")(x, ot, gt, w_out.astype(BF16))


def _rwkv_layer(x, norm_a, mu, w_in, w0, w2, a0, a2, kkw, kaw, rk, gn_w, gn_b, w_out):
    r, ld, k, v, kk, nb, bonus, g = _rwkv_pre(x, norm_a, mu, w_in, w0, w2, a0, a2, kkw, kaw, rk.reshape(-1))
    qh, yi, gm, cm = _rwkv_chunks(r, ld, k, v, kk, nb)
    return _rwkv_out(qh, yi, gm, cm, x, bonus, g, gn_w, gn_b, w_out)


def _moba_layer(x, norm_kv, w_kv, k_norm, norm_b, w_in, q_norm, w_out):
    B, T, D = x.shape
    k, kmean, qt, vt, g = _moba_pre(x, norm_kv, norm_b, w_kv, w_in, k_norm, q_norm)
    o = _moba_attn(k, kmean.transpose(0, 2, 1, 3), qt, vt)
    return _moba_post(x, o, g, w_out)


def kernel(x, norm_a, mu_a, w_in_a, w0_a, w2_a, a0_a, a2_a, kk_a, ka_a, rk_a, gn_w_a, gn_b_a, w_out_a, norm_kv, w_kv, k_norm, norm_b, w_in_b, q_norm_b, w_out_b):
    x = _rwkv_layer(x, norm_a[0], mu_a[0], w_in_a[0], w0_a[0], w2_a[0], a0_a[0], a2_a[0], kk_a[0], ka_a[0],
                    rk_a[0], gn_w_a[0], gn_b_a[0], w_out_a[0])
    return _moba_layer(x, norm_kv, w_kv, k_norm, norm_b[0], w_in_b[0], q_norm_b[0], w_out_b[0])
```

```python
import functools

import jax
import jax.numpy as jnp
from jax import lax
from jax.experimental import pallas as pl
from jax.experimental.pallas import tpu as pltpu

F32 = jnp.float32
BF16 = jnp.bfloat16

D_MODEL = 1024
HEAD_DIM = 64
N_HEADS = D_MODEL // HEAD_DIM
LANES = 128
N_PAIRS = D_MODEL // LANES
LORA = 64
MOBA_BLOCK = 256
MOBA_TOPK = 3
ROPE_THETA = 500000.0
ROPE_DIM = HEAD_DIM // 4
ROPE_HALF = ROPE_DIM // 2
RMS_EPS = 1e-6
GN_EPS = 64e-5
NEG = -1e30
LOG2E = 1.4426950408889634
SUM_ROWS = 16
TAG_LANES = LANES - HEAD_DIM
ATTN_GROUP = 2
ATTN_STAGES = 4
GATE_GROUP = 4
CHUNK = 64
VMEM_LIMIT = 56 * 1024 * 1024

NN = (((1,), (0,)), ((), ()))
NT = (((1,), (1,)), ((), ()))
TN = (((0,), (0,)), ((), ()))


def _mm(a, b, dims=NN):
    return lax.dot_general(a, b, dims, preferred_element_type=F32)


def _split(x):
    hi = x.astype(BF16)
    lo = (x - hi.astype(F32)).astype(BF16)
    return hi, lo


def _bf(x):
    return x.astype(BF16)


def _head_ones():
    r = lax.broadcasted_iota(jnp.int32, (LANES, LANES), 0) // HEAD_DIM
    c = lax.broadcasted_iota(jnp.int32, (LANES, LANES), 1) // HEAD_DIM
    return jnp.where(r == c, 1.0, 0.0).astype(BF16)


def _head_sum(x, ones):
    parts = [_mm(_bf(x[:, p * LANES:(p + 1) * LANES]), ones) for p in range(x.shape[1] // LANES)]
    return jnp.concatenate(parts, axis=1)


def _sigmoid(x):
    return 1.0 / (1.0 + jnp.exp(-x))


def _rms(x, g):
    return x * lax.rsqrt(jnp.mean(x * x, axis=-1, keepdims=True) + RMS_EPS) * g


def _params(*sem):
    return pltpu.CompilerParams(dimension_semantics=sem, vmem_limit_bytes=VMEM_LIMIT)


def _full(shape):
    return pl.BlockSpec(shape, lambda *_: (0,) * len(shape))


def _rwkv_pre_body(x_ref, xp_ref, na_ref, mu_ref, wr_ref, wwd_ref, wk_ref, wv_ref, wad_ref, wg_ref,
                   w0_ref, w2_ref, a0_ref, a2_ref, kkw_ref, kaw_ref, rk_ref,
                   r_ref, ld_ref, k_ref, v_ref, kk_ref, nb_ref, bonus_ref, g_ref):
    x = x_ref[0]
    gn = na_ref[...]
    h = _rms(x, gn)
    hp = _rms(xp_ref[0], gn)
    prev = jnp.where(pl.program_id(1) == 0, 0.0, hp[7:8, :])
    rolled = pltpu.roll(h, 1, axis=0)
    first = lax.broadcasted_iota(jnp.int32, h.shape, 0) == 0
    xx = jnp.where(first, prev, rolled) - h

    def mix(i):
        return (h + xx * mu_ref[i:i + 1, :]).astype(BF16)

    r = _mm(mix(0), wr_ref[...])
    wdn = _mm(mix(1), wwd_ref[...])
    k = _mm(mix(2), wk_ref[...])
    v = _mm(mix(3), wv_ref[...])
    adn = _mm(mix(4), wad_ref[...])
    gp = _mm(mix(5), wg_ref[...])

    z = -(w0_ref[...] + _mm(jnp.tanh(wdn).astype(BF16), w2_ref[...]))
    softplus = jnp.maximum(z, 0.0) + jnp.log(1.0 + jnp.exp(-jnp.abs(z)))
    ld_ref[0] = -jnp.exp(-softplus - 0.5)
    a = _sigmoid(a0_ref[...] + _mm(adn.astype(BF16), a2_ref[...]))

    ones = _head_ones()
    kk = k * kkw_ref[...]
    kk = kk / jnp.maximum(jnp.sqrt(_head_sum(kk * kk, ones)), 1e-12)
    k2 = k * (1.0 + (a - 1.0) * kaw_ref[...])
    r_ref[0] = r
    k_ref[0] = k2
    v_ref[0] = v
    kk_ref[0] = kk
    nb_ref[0] = -(kk * a)
    bonus_ref[0] = _bf(_head_sum(r * k2 * rk_ref[...], ones) * v)
    g_ref[0] = _bf(gp * _sigmoid(gp))


def _rwkv_pre(x, norm_a, mu, w_in, w0, w2, a0, a2, kkw, kaw, rk, tm=256):
    B, T, D = x.shape
    offs = [0, D, D + LORA, 2 * D + LORA, 3 * D + LORA, 3 * D + 2 * LORA, 4 * D + 2 * LORA]
    ws = [w_in[:, offs[i]:offs[i + 1]].astype(BF16) for i in range(6)]
    row = lambda a: a.reshape(1, D)
    big = pl.BlockSpec((1, tm, D), lambda b, i: (b, i, 0))
    prev = pl.BlockSpec((1, 8, D), lambda b, i: (b, jnp.maximum(i * (tm // 8) - 1, 0), 0))
    ins = [x, x, row(norm_a), mu] + ws + [row(w0), w2.astype(BF16), row(a0), a2.astype(BF16),
                                          row(kkw), row(kaw), row(rk)]
    in_specs = [big, prev] + [_full(a.shape) for a in ins[2:]]
    out = lambda dt: jax.ShapeDtypeStruct((B, T, D), dt)
    return pl.pallas_call(
        _rwkv_pre_body, grid=(B, T // tm), in_specs=in_specs, out_specs=[big] * 8,
        out_shape=[out(F32)] * 6 + [out(BF16)] * 2,
        compiler_params=_params("parallel", "parallel"), name="rwkv_pre")(*ins)


def _stack(x, lane_lo):
    return jnp.concatenate([jnp.where(lane_lo, x, 0.0), jnp.where(lane_lo, 0.0, x)], axis=0)


def _unstack(z):
    half = z.shape[0] // 2
    return z[:half] + z[half:]


def _rwkv_chunk_body(r_ref, ld_ref, k_ref, v_ref, kk_ref, nb_ref, qh_ref, yi_ref, g_ref, c_ref, *, nc):
    L = CHUNK
    S = 2 * L
    lane_lo = lax.broadcasted_iota(jnp.int32, (L, LANES), 1) < HEAD_DIM
    ti = lax.broadcasted_iota(jnp.int32, (L, L), 0)
    tj = lax.broadcasted_iota(jnp.int32, (L, L), 1)
    tri = jnp.where(tj <= ti, 1.0, 0.0).astype(BF16)
    pi = lax.broadcasted_iota(jnp.int32, (2 * S, 2 * S), 0)
    pj = lax.broadcasted_iota(jnp.int32, (2 * S, 2 * S), 1)
    same_head = ((pi >> 6) & 1) == ((pj >> 6) & 1)
    t_i, s_j = pi & (L - 1), pj & (L - 1)
    valid = same_head & ((s_j < t_i) | ((pi >= S) & (s_j == t_i)))
    ei = lax.broadcasted_iota(jnp.int32, (S, S), 0)
    ej = lax.broadcasted_iota(jnp.int32, (S, S), 1)
    eye = ei == ej
    stack = lambda x: _bf(_stack(x, lane_lo))

    cs = range(nc)
    each = lambda f, *xs: [f(*a) for a in zip(*xs)]
    load = lambda ref: [ref[0, c * L:(c + 1) * L, :] for c in cs]
    r, ld, k, v, kk, nb = (load(ref) for ref in (r_ref, ld_ref, k_ref, v_ref, kk_ref, nb_ref))

    def cumsum(x):
        x1, x2 = _split(x)
        x3 = (x - x1.astype(F32) - x2.astype(F32)).astype(BF16)
        return _mm(tri, x1) + (_mm(tri, x2) + _mm(tri, x3))

    cum = each(cumsum, ld)
    cref = [x[L // 2 - 1:L // 2, :] for x in cum]
    clast = [x[L - 1:L, :] for x in cum]
    ecum = each(lambda a, b: a - b, cum, cref)
    e_in = [jnp.exp(x) for x in ecum]
    e_ex = each(lambda a, b: jnp.exp(a - b), ecum, ld)
    e_neg = [jnp.exp(-x) for x in ecum]
    p_ref = [jnp.exp(x) for x in cref]
    e_last = each(lambda a, b: jnp.exp(a - b), clast, cref)
    mul = lambda a, b: a * b
    q_m, q_n = each(mul, kk, e_ex), each(mul, r, e_in)
    k_b, k_k = each(mul, nb, e_neg), each(mul, k, e_neg)
    lhs = each(lambda a, b: jnp.concatenate([stack(a), stack(b)], axis=0), q_m, q_n)
    rhs = each(lambda a, b: jnp.concatenate([stack(a), stack(b)], axis=0), k_b, k_k)
    pm = each(lambda a, b: jnp.where(valid, _mm(a, b, NT), 0.0), lhs, rhs)
    m_b = [x[:S, :S] for x in pm]
    m_k, n_b, n_k = ([_bf(x[rs, cs_]) for x in pm] for rs, cs_ in
                     ((slice(0, S), slice(S, 2 * S)), (slice(S, 2 * S), slice(0, S)), (slice(S, 2 * S), slice(S, 2 * S))))
    tm = [jnp.where(eye, 1.0, 0.0) + x for x in m_b]
    xp = m_b
    for _ in range(5):
        xp = [_mm(xb, xb) for xb in (_bf(x) for x in xp)]
        tm = each(lambda t, x: t + _mm(_bf(t), _bf(x)), tm, xp)
    sv = [stack(x) for x in v]
    mkv = each(lambda a, b: _bf(_mm(a, b)), m_k, sv)
    qm0 = each(lambda a, b: stack(a * b), q_m, p_ref)
    wu = each(lambda t, a, b: _bf(_mm(_bf(t), jnp.concatenate([a, b], axis=1))), tm, qm0, mkv)
    kbl_t = each(lambda a, b: _bf(_stack(a * b, lane_lo).T), k_b, e_last)
    kkl_t = each(lambda a, b: _bf(_stack(a * b, lane_lo).T), k_k, e_last)
    gc = each(_mm, kbl_t, wu)
    kv = each(_mm, kkl_t, sv)
    nw = each(_mm, n_b, wu)
    nv = each(_mm, n_k, sv)
    for c in cs:
        sl = slice(c * L, (c + 1) * L)
        qh_ref[0, sl, :] = r[c] * (e_in[c] * p_ref[c]) + _unstack(nw[c][:, :S])
        yi_ref[0, sl, :] = _unstack(nw[c][:, S:] + nv[c])
        g_ref[0, 0, c] = _unstack(gc[c][:, :S] + jnp.where(eye, jnp.exp(clast[c]), 0.0))
        c_ref[0, 0, c] = _unstack(gc[c][:, S:] + kv[c])


def _rwkv_chunks(r, ld, k, v, kk, nb, nc=16):
    B, T, D = r.shape
    L = CHUNK
    tt = nc * L
    nch = T // L
    nat = pl.BlockSpec((1, tt, LANES), lambda b, p, i: (b, i, p))
    op = pl.BlockSpec((1, 1, nc, L, LANES), lambda b, p, i: (b, p, i, 0, 0))
    nat_shape = jax.ShapeDtypeStruct((B, T, D), F32)
    op_shape = jax.ShapeDtypeStruct((B, N_PAIRS, nch, L, LANES), F32)
    return pl.pallas_call(
        functools.partial(_rwkv_chunk_body, nc=nc), grid=(B, N_PAIRS, T // tt),
        in_specs=[nat] * 6, out_specs=[nat, nat, op, op], out_shape=[nat_shape, nat_shape, op_shape, op_shape],
        compiler_params=_params("parallel", "parallel", "parallel"), name="rwkv_chunks")(r, ld, k, v, kk, nb)


def _rwkv_out_body(qh_ref, yi_ref, g_ref, c_ref, x_ref, bonus_ref, gate_ref, gw_ref, gb_ref, wo_ref,
                   o_ref, h_ref, y_ref, *, nc):
    L = CHUNK
    lane_lo = lax.broadcasted_iota(jnp.int32, (L, LANES), 1) < HEAD_DIM

    @pl.when(pl.program_id(1) == 0)
    def _():
        h_ref[...] = jnp.zeros_like(h_ref)

    def chunk(c, hs):
        sl = pl.ds(pl.multiple_of(c * L, L), L)
        out = []
        for p in range(N_PAIRS):
            lanes = slice(p * LANES, (p + 1) * LANES)
            lhs = jnp.concatenate([qh_ref[0, sl, lanes], _stack(g_ref[0, p, c], lane_lo)], axis=0)
            prod = _mm(_bf(lhs), _bf(hs[p]))
            y_ref[sl, lanes] = prod[:L] + yi_ref[0, sl, lanes]
            out.append(prod[L:] + _stack(c_ref[0, p, c], lane_lo))
        return tuple(out)

    hs = lax.fori_loop(0, nc, chunk, tuple(h_ref[p] for p in range(N_PAIRS)))
    for p in range(N_PAIRS):
        h_ref[p] = hs[p]

    ones = _head_ones()
    y = y_ref[...]
    mean = _head_sum(y, ones) * (1.0 / HEAD_DIM)
    d = y - mean
    var = _head_sum(d * d, ones) * (1.0 / HEAD_DIM)
    yn = d * lax.rsqrt(var + GN_EPS) * gw_ref[...] + gb_ref[...]
    z = _bf((yn + bonus_ref[0].astype(F32)) * gate_ref[0].astype(F32))
    o_ref[0] = x_ref[0] + _mm(z, wo_ref[...])


def _rwkv_out(qh, yi, g, c, x, bonus, gate, gn_w, gn_b, w_out, nc=8):
    B, T, D = qh.shape
    L = CHUNK
    tt = nc * L
    big = pl.BlockSpec((1, tt, D), lambda b, i: (b, i, 0))
    op = pl.BlockSpec((1, N_PAIRS, nc, L, LANES), lambda b, i: (b, 0, i, 0, 0))
    ins = [qh, yi, g, c, x, bonus, gate, gn_w.reshape(1, D), gn_b.reshape(1, D), w_out.astype(BF16)]
    return pl.pallas_call(
        functools.partial(_rwkv_out_body, nc=nc), grid=(B, T // tt),
        in_specs=[big, big, op, op, big, big, big] + [_full(a.shape) for a in ins[7:]],
        out_specs=big, out_shape=jax.ShapeDtypeStruct((B, T, D), F32),
        scratch_shapes=[pltpu.VMEM((N_PAIRS, 2 * HEAD_DIM, LANES), F32), pltpu.VMEM((tt, D), F32)],
        compiler_params=_params("parallel", "arbitrary"), name="rwkv_out")(*ins)


def _head_norm_rope_t(xt, gain, cos, sin):
    heads = []
    for h in range(N_HEADS):
        xh = xt[h * HEAD_DIM:(h + 1) * HEAD_DIM, :]
        xh = xh * lax.rsqrt(jnp.mean(xh * xh, axis=0, keepdims=True) + RMS_EPS) * gain
        x1, x2 = xh[:ROPE_HALF], xh[ROPE_HALF:ROPE_DIM]
        heads += [x1 * cos - x2 * sin, x2 * cos + x1 * sin, xh[ROPE_DIM:]]
    return jnp.concatenate(heads, axis=0)


def _moba_pre_body(x_ref, nkv_ref, nb_ref, wkv_ref, wq_ref, wg_ref, kn_ref, qn_ref, cos_ref, sin_ref,
                   k_ref, km_ref, qt_ref, vt_ref, g_ref):
    x = x_ref[0]
    xn = x * lax.rsqrt(jnp.mean(x * x, axis=-1, keepdims=True) + RMS_EPS)
    h_kv = (xn * nkv_ref[...]).astype(BF16)
    h_b = (xn * nb_ref[...]).astype(BF16)
    cos, sin = cos_ref[...], sin_ref[...]
    tile = lambda g: jnp.concatenate([g] * (x.shape[0] // LANES), axis=1)
    kvt = _mm(wkv_ref[...], h_kv, NT)
    vt_ref[0, 0] = kvt[D_MODEL:].astype(BF16)
    kt = _head_norm_rope_t(kvt[:D_MODEL], tile(kn_ref[...]), cos, sin)
    tag = lax.broadcasted_iota(jnp.int32, (TAG_LANES, x.shape[0]), 0) == pl.program_id(1)
    tag = jnp.where(tag, 1.0, 0.0)
    for h in range(N_HEADS):
        k_aug = jnp.concatenate([kt[h * HEAD_DIM:(h + 1) * HEAD_DIM], tag], axis=0).T
        k_ref[0, h] = k_aug.astype(BF16)
        km_ref[0, 0, h:h + 1, :] = jnp.mean(k_aug, axis=0, keepdims=True)
    qt = _head_norm_rope_t(_mm(wq_ref[...], h_b, NT), tile(qn_ref[...]), cos, sin)
    qt_ref[0, 0] = (qt * (LOG2E / HEAD_DIM ** 0.5)).astype(BF16)
    gate = _mm(wg_ref[...], h_b, NT)
    g_ref[0, 0] = _bf(gate * _sigmoid(gate))


def _rope_tables_t(T):
    inv_freq = jnp.power(jnp.float32(ROPE_THETA), -jnp.arange(0, ROPE_DIM, 2, dtype=F32) / ROPE_DIM)
    ang = inv_freq[:, None] * jnp.arange(T, dtype=F32)[None, :]
    return jnp.cos(ang), jnp.sin(ang)


def _moba_pre(x, norm_kv, norm_b, w_kv, w_in, k_norm, q_norm):
    B, T, D = x.shape
    tm = MOBA_BLOCK
    nb = T // tm
    cos, sin = _rope_tables_t(T)
    bcast = lambda g: jnp.broadcast_to(g.reshape(HEAD_DIM, 1), (HEAD_DIM, LANES))
    ins = [x, norm_kv.reshape(1, D), norm_b.reshape(1, D), w_kv.T.astype(BF16), w_in[:, :D].T.astype(BF16),
           w_in[:, D:].T.astype(BF16), bcast(k_norm), bcast(q_norm), cos, sin]
    big = pl.BlockSpec((1, tm, D), lambda b, i: (b, i, 0))
    tab = pl.BlockSpec((ROPE_HALF, tm), lambda b, i: (0, i))
    tr = pl.BlockSpec((1, 1, D, tm), lambda b, i: (b, i, 0, 0))
    in_specs = [big] + [_full(a.shape) for a in ins[1:8]] + [tab, tab]
    assert tm == MOBA_BLOCK and nb <= TAG_LANES
    out_specs = [pl.BlockSpec((1, N_HEADS, tm, LANES), lambda b, i: (b, 0, i, 0)),
                 pl.BlockSpec((1, 1, N_HEADS, LANES), lambda b, i: (b, i, 0, 0)), tr, tr, tr]
    out_shape = [jax.ShapeDtypeStruct((B, N_HEADS, T, LANES), BF16), jax.ShapeDtypeStruct((B, nb, N_HEADS, LANES), F32),
                 jax.ShapeDtypeStruct((B, nb, D, tm), BF16), jax.ShapeDtypeStruct((B, nb, D, tm), BF16),
                 jax.ShapeDtypeStruct((B, nb, D, tm), BF16)]
    return pl.pallas_call(
        _moba_pre_body, grid=(B, nb), in_specs=in_specs, out_specs=out_specs, out_shape=out_shape,
        compiler_params=_params("parallel", "parallel"), name="moba_pre")(*ins)


def _attn_schedule(nb):
    items = []
    for d in range(1, nb):
        diag = [(j, j + d) for j in range(nb - d)]
        diag += [(0, nb)] * (-len(diag) % ATTN_GROUP)
        items += diag
    n_trips = -(-len(items) // (ATTN_GROUP * ATTN_STAGES))
    items += [(0, nb)] * (ATTN_GROUP * (n_trips * ATTN_STAGES + 1) - len(items))
    return [v for it in items for v in it], n_trips


def _moba_attn_body(tab_ref, k_ref, km_ref, qt_ref, vt_ref, o_ref, qa_ref, acc_ref, m_ref, sa_ref, sb_ref,
                    *, nb, n_trips):
    BS = MOBA_BLOCK
    G = ATTN_GROUP
    heads = range(2)
    ones = jnp.ones((SUM_ROWS, BS), BF16)
    zeros_tag = jnp.zeros((TAG_LANES, BS), BF16)
    rows = lambda h: slice(h * HEAD_DIM, (h + 1) * HEAD_DIM)
    v_aug = lambda j, h: jnp.concatenate([vt_ref[0, j, rows(h), :], ones], axis=0)
    k_blk = lambda j, h: k_ref[0, h, pl.ds(pl.multiple_of(j * BS, BS), BS), :]

    blk_f = lax.broadcasted_iota(jnp.int32, (nb, BS), 0).astype(F32)
    km = [_split(km_ref[0, h]) for h in heads]
    pad = [jnp.full((TAG_LANES - nb, BS), NEG, F32)] if nb < TAG_LANES else []

    def gate(g, carry):
        ih = [(g * GATE_GROUP + e, h) for e in range(GATE_GROUP) for h in heads]
        q = [qt_ref[0, i, rows(h), :] for i, h in ih]
        qz = [jnp.concatenate([x, zeros_tag], axis=0) for x in q]
        gs = [_mm(km[h][0], x) + _mm(km[h][1], x) for x, (i, h) in zip(qz, ih)]
        work = [jnp.where(blk_f < i.astype(F32), x, -jnp.inf) for x, (i, h) in zip(gs, ih)]
        f = [jnp.full((nb, BS), NEG, F32) for _ in ih]
        for _ in range(MOBA_TOPK):
            mx = [jnp.max(x, axis=0, keepdims=True) for x in work]
            first = [jnp.min(jnp.where(w == m, blk_f, float(nb)), axis=0, keepdims=True) for w, m in zip(work, mx)]
            hit = [blk_f == x for x in first]
            f = [jnp.where(h_, jnp.where(m > -jnp.inf, 0.0, f_), f_) for h_, m, f_ in zip(hit, mx, f)]
            work = [jnp.where(h_, -jnp.inf, w) for h_, w in zip(hit, work)]
        for x, f_, (i, h) in zip(q, f, ih):
            qa_ref[i, h] = jnp.concatenate([x, _bf(jnp.concatenate([f_] + pad, axis=0))], axis=0)
        return carry

    lax.fori_loop(0, nb // GATE_GROUP, gate, 0)
    for h in heads:
        qa_ref[nb, h] = jnp.concatenate([jnp.zeros((HEAD_DIM, BS), BF16), jnp.full((TAG_LANES, BS), NEG, BF16)], axis=0)
        acc_ref[nb, h] = jnp.zeros(acc_ref.shape[2:], F32)
        m_ref[nb, h] = jnp.full(m_ref.shape[2:], 0.5 * NEG, F32)

    key_pos = lax.broadcasted_iota(jnp.int32, (BS, BS), 0)
    qry_pos = lax.broadcasted_iota(jnp.int32, (BS, BS), 1)

    def own(g, carry):
        ih = [(g * G + e, h) for e in range(G) for h in heads]
        s = [jnp.where(key_pos <= qry_pos,
                       _mm(k_blk(i, h), jnp.concatenate([qt_ref[0, i, rows(h), :], zeros_tag], axis=0)), NEG)
             for i, h in ih]
        m0 = [jnp.max(x, axis=0, keepdims=True) for x in s]
        acc = [_mm(v_aug(i, h), _bf(jnp.exp2(x - m))) for x, m, (i, h) in zip(s, m0, ih)]
        for m, a, (i, h) in zip(m0, acc, ih):
            m_ref[i, h] = m
            acc_ref[i, h] = a
        return carry

    lax.fori_loop(0, nb // G, own, 0)

    eh = [(e, h) for e in range(G) for h in heads]
    item = lambda g, e: (tab_ref[2 * (g * G + e)], tab_ref[2 * (g * G + e) + 1])
    def scores(g, dst_ref):
        for c, (e, h) in enumerate(eh):
            j, i = item(g, e)
            dst_ref[c] = _mm(k_blk(j, h), qa_ref[i, h])

    def consume(src_ref, g):
        for c, (e, h) in enumerate(eh):
            j, i = item(g, e)
            s = src_ref[c]
            m_old = m_ref[i, h]
            m_new = jnp.maximum(m_old, jnp.max(s, axis=0, keepdims=True))
            acc = jnp.exp2(m_old - m_new) * acc_ref[i, h] + _mm(v_aug(j, h), _bf(jnp.exp2(s - m_new)))
            m_ref[i, h] = m_new
            acc_ref[i, h] = acc

    scores(0, sa_ref)

    def trip(t, carry):
        g0 = t * ATTN_STAGES
        for u in range(0, ATTN_STAGES, 2):
            scores(g0 + u + 1, sb_ref)
            consume(sa_ref, g0 + u)
            scores(g0 + u + 2, sa_ref)
            consume(sb_ref, g0 + u + 1)
        return carry

    lax.fori_loop(0, n_trips, trip, 0)

    def finish(g, carry):
        for e in range(G):
            i = g * G + e
            acc = [acc_ref[i, h] for h in heads]
            o_ref[0, i] = _bf(jnp.concatenate([a[:HEAD_DIM] / a[HEAD_DIM:HEAD_DIM + 1] for a in acc], axis=0))
        return carry

    lax.fori_loop(0, nb // G, finish, 0)


def _moba_attn(k, kmean, qt, vt):
    B, H, T, _ = k.shape
    BS = MOBA_BLOCK
    nb = T // BS
    assert nb % GATE_GROUP == 0 and nb % ATTN_GROUP == 0 and ATTN_STAGES % 2 == 0
    table, n_trips = _attn_schedule(nb)
    in_specs = [pl.BlockSpec(memory_space=pltpu.SMEM),
                pl.BlockSpec((1, 2, T, LANES), lambda b, p: (b, p, 0, 0)),
                pl.BlockSpec((1, 2, nb, LANES), lambda b, p: (b, p, 0, 0)),
                pl.BlockSpec((1, nb, LANES, BS), lambda b, p: (b, 0, p, 0)),
                pl.BlockSpec((1, nb, LANES, BS), lambda b, p: (b, 0, p, 0))]
    chains = 2 * ATTN_GROUP
    scratch = [pltpu.VMEM((nb + 1, 2, LANES, BS), BF16),
               pltpu.VMEM((nb + 1, 2, HEAD_DIM + SUM_ROWS, BS), F32),
               pltpu.VMEM((nb + 1, 2, 1, BS), F32),
               pltpu.VMEM((chains, BS, BS), F32), pltpu.VMEM((chains, BS, BS), F32)]
    return pl.pallas_call(
        functools.partial(_moba_attn_body, nb=nb, n_trips=n_trips), grid=(B, N_PAIRS), in_specs=in_specs,
        out_specs=pl.BlockSpec((1, nb, LANES, BS), lambda b, p: (b, 0, p, 0)),
        out_shape=jax.ShapeDtypeStruct((B, nb, H * HEAD_DIM, BS), BF16), scratch_shapes=scratch,
        compiler_params=_params("parallel", "parallel"), name="moba_attn")(
            jnp.asarray(table, jnp.int32), k, kmean, qt, vt)


def _moba_post_body(x_ref, ot_ref, gt_ref, wo_ref, out_ref):
    BS = MOBA_BLOCK
    for u in range(ot_ref.shape[1]):
        zt = _bf(ot_ref[0, u].astype(F32) * gt_ref[0, u].astype(F32))
        out_ref[0, u * BS:(u + 1) * BS, :] = x_ref[0, u * BS:(u + 1) * BS, :] + _mm(zt, wo_ref[...], TN)


def _moba_post(x, ot, gt, w_out, tm=512):
    B, T, D = x.shape
    big = pl.BlockSpec((1, tm, D), lambda b, i: (b, i, 0))
    tr = pl.BlockSpec((1, tm // MOBA_BLOCK, D, MOBA_BLOCK), lambda b, i: (b, i, 0, 0))
    return pl.pallas_call(
        _moba_post_body, grid=(B, T // tm), in_specs=[big, tr, tr, _full((D, D))], out_specs=big,
        out_shape=jax.ShapeDtypeStruct((B, T, D), F32),
        compiler_params=_params("parallel", "parallel"), name="moba_post")(x, ot, gt, w_out.astype(BF16))


def _rwkv_layer(x, norm_a, mu, w_in, w0, w2, a0, a2, kkw, kaw, rk, gn_w, gn_b, w_out):
    r, ld, k, v, kk, nb, bonus, g = _rwkv_pre(x, norm_a, mu, w_in, w0, w2, a0, a2, kkw, kaw, rk.reshape(-1))
    qh, yi, gm, cm = _rwkv_chunks(r, ld, k, v, kk, nb)
    return _rwkv_out(qh, yi, gm, cm, x, bonus, g, gn_w, gn_b, w_out)


def _moba_layer(x, norm_kv, w_kv, k_norm, norm_b, w_in, q_norm, w_out):
    B, T, D = x.shape
    k, kmean, qt, vt, g = _moba_pre(x, norm_kv, norm_b, w_kv, w_in, k_norm, q_norm)
    o = _moba_attn(k, kmean.transpose(0, 2, 1, 3), qt, vt)
    return _moba_post(x, o, g, w_out)


def kernel(x, norm_a, mu_a, w_in_a, w0_a, w2_a, a0_a, a2_a, kk_a, ka_a, rk_a, gn_w_a, gn_b_a, w_out_a, norm_kv, w_kv, k_norm, norm_b, w_in_b, q_norm_b, w_out_b):
    x = _rwkv_layer(x, norm_a[0], mu_a[0], w_in_a[0], w0_a[0], w2_a[0], a0_a[0], a2_a[0], kk_a[0], ka_a[0],
                    rk_a[0], gn_w_a[0], gn_b_a[0], w_out_a[0])
    return _moba_layer(x, norm_kv, w_kv, k_norm, norm_b[0], w_in_b[0], q_norm_b[0], w_out_b[0])
```

```python
import functools

import jax
import jax.numpy as jnp
from jax import lax
from jax.experimental import pallas as pl
from jax.experimental.pallas import tpu as pltpu

F32 = jnp.float32
BF16 = jnp.bfloat16

D_MODEL = 1024
HEAD_DIM = 64
N_HEADS = D_MODEL // HEAD_DIM
LANES = 128
N_PAIRS = D_MODEL // LANES
LORA = 64
MOBA_BLOCK = 256
MOBA_TOPK = 3
ROPE_THETA = 500000.0
ROPE_DIM = HEAD_DIM // 4
ROPE_HALF = ROPE_DIM // 2
RMS_EPS = 1e-6
GN_EPS = 64e-5
NEG = -1e30
LOG2E = 1.4426950408889634
DECAY_SCALE = 0.6065306597126334
SUM_ROWS = 16
TAG_LANES = LANES - HEAD_DIM
ATTN_GROUP = 2
ATTN_STAGES = 4
GATE_GROUP = 4
CHUNK = 64
VMEM_LIMIT = 56 * 1024 * 1024

NN = (((1,), (0,)), ((), ()))
NT = (((1,), (1,)), ((), ()))
TN = (((0,), (0,)), ((), ()))


def _mm(a, b, dims=NN):
    return lax.dot_general(a, b, dims, preferred_element_type=F32)


def _split(x):
    hi = x.astype(BF16)
    lo = (x - hi.astype(F32)).astype(BF16)
    return hi, lo


def _bf(x):
    return x.astype(BF16)


def _head_ones():
    r = lax.broadcasted_iota(jnp.int32, (LANES, LANES), 0) // HEAD_DIM
    c = lax.broadcasted_iota(jnp.int32, (LANES, LANES), 1) // HEAD_DIM
    return jnp.where(r == c, 1.0, 0.0).astype(BF16)


def _head_sum(x, ones):
    parts = [_mm(_bf(x[:, p * LANES:(p + 1) * LANES]), ones) for p in range(x.shape[1] // LANES)]
    return jnp.concatenate(parts, axis=1)


def _sigmoid(x):
    return 1.0 / (1.0 + jnp.exp(-x))


def _rms(x, g):
    return x * lax.rsqrt(jnp.mean(x * x, axis=-1, keepdims=True) + RMS_EPS) * g


def _params(*sem):
    return pltpu.CompilerParams(dimension_semantics=sem, vmem_limit_bytes=VMEM_LIMIT)


def _full(shape):
    return pl.BlockSpec(shape, lambda *_: (0,) * len(shape))


def _rwkv_pre_body(x_ref, xp_ref, na_ref, mu_ref, wr_ref, wwd_ref, wk_ref, wv_ref, wad_ref, wg_ref,
                   w0_ref, w2_ref, a0_ref, a2_ref, kkw_ref, kaw_ref, rk_ref,
                   r_ref, ld_ref, k_ref, v_ref, kk_ref, nb_ref, bonus_ref, g_ref):
    x = x_ref[0]
    gn = na_ref[...]
    h = _rms(x, gn)
    hp = _rms(xp_ref[0], gn)
    prev = jnp.where(pl.program_id(1) == 0, 0.0, hp[7:8, :])
    rolled = pltpu.roll(h, 1, axis=0)
    first = lax.broadcasted_iota(jnp.int32, h.shape, 0) == 0
    xx = jnp.where(first, prev, rolled) - h

    def mix(i):
        return (h + xx * mu_ref[i:i + 1, :]).astype(BF16)

    r = _mm(mix(0), wr_ref[...])
    wdn = _mm(mix(1), wwd_ref[...])
    k = _mm(mix(2), wk_ref[...])
    v = _mm(mix(3), wv_ref[...])
    adn = _mm(mix(4), wad_ref[...])
    gp = _mm(mix(5), wg_ref[...])

    ld_ref[0] = -DECAY_SCALE * _sigmoid(w0_ref[...] + _mm(jnp.tanh(wdn).astype(BF16), w2_ref[...]))
    a = _sigmoid(a0_ref[...] + _mm(adn.astype(BF16), a2_ref[...]))

    ones = _head_ones()
    kk = k * kkw_ref[...]
    kk = kk * lax.rsqrt(jnp.maximum(_head_sum(kk * kk, ones), 1e-24))
    k2 = k * (1.0 + (a - 1.0) * kaw_ref[...])
    r_ref[0] = r
    k_ref[0] = k2
    v_ref[0] = v
    kk_ref[0] = kk
    nb_ref[0] = -(kk * a)
    bonus_ref[0] = _bf(_head_sum(r * k2 * rk_ref[...], ones) * v)
    g_ref[0] = _bf(gp * _sigmoid(gp))


def _rwkv_pre(x, norm_a, mu, w_in, w0, w2, a0, a2, kkw, kaw, rk, tm=256):
    B, T, D = x.shape
    offs = [0, D, D + LORA, 2 * D + LORA, 3 * D + LORA, 3 * D + 2 * LORA, 4 * D + 2 * LORA]
    ws = [w_in[:, offs[i]:offs[i + 1]].astype(BF16) for i in range(6)]
    row = lambda a: a.reshape(1, D)
    big = pl.BlockSpec((1, tm, D), lambda b, i: (b, i, 0))
    prev = pl.BlockSpec((1, 8, D), lambda b, i: (b, jnp.maximum(i * (tm // 8) - 1, 0), 0))
    ins = [x, x, row(norm_a), mu] + ws + [row(w0), w2.astype(BF16), row(a0), a2.astype(BF16),
                                          row(kkw), row(kaw), row(rk)]
    in_specs = [big, prev] + [_full(a.shape) for a in ins[2:]]
    out = lambda dt: jax.ShapeDtypeStruct((B, T, D), dt)
    return pl.pallas_call(
        _rwkv_pre_body, grid=(B, T // tm), in_specs=in_specs, out_specs=[big] * 8,
        out_shape=[out(F32)] * 6 + [out(BF16)] * 2,
        compiler_params=_params("parallel", "parallel"), name="rwkv_pre")(*ins)


def _stack(x, lane_lo):
    return jnp.concatenate([jnp.where(lane_lo, x, 0.0), jnp.where(lane_lo, 0.0, x)], axis=0)


def _unstack(z):
    half = z.shape[0] // 2
    return z[:half] + z[half:]


def _rwkv_chunk_body(r_ref, ld_ref, k_ref, v_ref, kk_ref, nb_ref, qh_ref, yi_ref, g_ref, c_ref, *, nc):
    L = CHUNK
    S = 2 * L
    lane_lo = lax.broadcasted_iota(jnp.int32, (L, LANES), 1) < HEAD_DIM
    row = lax.broadcasted_iota(jnp.int32, (L, LANES), 0)
    pi = lax.broadcasted_iota(jnp.int32, (2 * S, 2 * S), 0)
    pj = lax.broadcasted_iota(jnp.int32, (2 * S, 2 * S), 1)
    same_head = ((pi >> 6) & 1) == ((pj >> 6) & 1)
    t_i, s_j = pi & (L - 1), pj & (L - 1)
    valid = same_head & ((s_j < t_i) | ((pi >= S) & (s_j == t_i)))
    ei = lax.broadcasted_iota(jnp.int32, (S, S), 0)
    ej = lax.broadcasted_iota(jnp.int32, (S, S), 1)
    eye = ei == ej
    stack = lambda x: _bf(_stack(x, lane_lo))

    cs = range(nc)
    each = lambda f, *xs: [f(*a) for a in zip(*xs)]
    load = lambda ref: [ref[0, c * L:(c + 1) * L, :] for c in cs]
    r, ld, k, v, kk, nb = (load(ref) for ref in (r_ref, ld_ref, k_ref, v_ref, kk_ref, nb_ref))

    def cumsum(x):
        d = 1
        while d < L:
            x = x + jnp.where(row >= d, pltpu.roll(x, d, axis=0), 0.0)
            d *= 2
        return x

    cum = each(cumsum, ld)
    cref = [x[L // 2 - 1:L // 2, :] for x in cum]
    clast = [x[L - 1:L, :] for x in cum]
    ecum = each(lambda a, b: a - b, cum, cref)
    e_in = [jnp.exp(x) for x in ecum]
    e_ex = each(lambda a, b: jnp.exp(a - b), ecum, ld)
    e_neg = [jnp.exp(-x) for x in ecum]
    p_ref = [jnp.exp(x) for x in cref]
    e_last = each(lambda a, b: jnp.exp(a - b), clast, cref)
    mul = lambda a, b: a * b
    q_m, q_n = each(mul, kk, e_ex), each(mul, r, e_in)
    k_b, k_k = each(mul, nb, e_neg), each(mul, k, e_neg)
    lhs = each(lambda a, b: jnp.concatenate([stack(a), stack(b)], axis=0), q_m, q_n)
    rhs = each(lambda a, b: jnp.concatenate([stack(a), stack(b)], axis=0), k_b, k_k)
    pm = each(lambda a, b: jnp.where(valid, _mm(a, b, NT), 0.0), lhs, rhs)
    m_b = [x[:S, :S] for x in pm]
    m_k, n_b, n_k = ([_bf(x[rs, cs_]) for x in pm] for rs, cs_ in
                     ((slice(0, S), slice(S, 2 * S)), (slice(S, 2 * S), slice(0, S)), (slice(S, 2 * S), slice(S, 2 * S))))
    tm = [jnp.where(eye, 1.0, 0.0) + x for x in m_b]
    xp = m_b
    for _ in range(5):
        xp = [_mm(xb, xb) for xb in (_bf(x) for x in xp)]
        tm = each(lambda t, x: t + _mm(_bf(t), _bf(x)), tm, xp)
    sv = [stack(x) for x in v]
    mkv = each(lambda a, b: _bf(_mm(a, b)), m_k, sv)
    qm0 = each(lambda a, b: stack(a * b), q_m, p_ref)
    wu = each(lambda t, a, b: _bf(_mm(_bf(t), jnp.concatenate([a, b], axis=1))), tm, qm0, mkv)
    kbl_t = each(lambda a, b: _bf(_stack(a * b, lane_lo).T), k_b, e_last)
    kkl_t = each(lambda a, b: _bf(_stack(a * b, lane_lo).T), k_k, e_last)
    gc = each(_mm, kbl_t, wu)
    kv = each(_mm, kkl_t, sv)
    nw = each(_mm, n_b, wu)
    nv = each(_mm, n_k, sv)
    for c in cs:
        sl = slice(c * L, (c + 1) * L)
        qh_ref[0, sl, :] = _bf(r[c] * (e_in[c] * p_ref[c]) + _unstack(nw[c][:, :S]))
        yi_ref[0, sl, :] = _unstack(nw[c][:, S:] + nv[c])
        g_ref[0, 0, c] = _bf(_unstack(gc[c][:, :S] + jnp.where(eye, jnp.exp(clast[c]), 0.0)))
        c_ref[0, 0, c] = _unstack(gc[c][:, S:] + kv[c])


def _rwkv_chunks(r, ld, k, v, kk, nb, nc=16):
    B, T, D = r.shape
    L = CHUNK
    tt = nc * L
    nch = T // L
    nat = pl.BlockSpec((1, tt, LANES), lambda b, p, i: (b, i, p))
    op = pl.BlockSpec((1, 1, nc, L, LANES), lambda b, p, i: (b, p, i, 0, 0))
    nat_shape = lambda dt: jax.ShapeDtypeStruct((B, T, D), dt)
    op_shape = lambda dt: jax.ShapeDtypeStruct((B, N_PAIRS, nch, L, LANES), dt)
    return pl.pallas_call(
        functools.partial(_rwkv_chunk_body, nc=nc), grid=(B, N_PAIRS, T // tt),
        in_specs=[nat] * 6, out_specs=[nat, nat, op, op],
        out_shape=[nat_shape(BF16), nat_shape(F32), op_shape(BF16), op_shape(F32)],
        compiler_params=_params("parallel", "parallel", "parallel"), name="rwkv_chunks")(r, ld, k, v, kk, nb)


def _rwkv_out_body(qh_ref, yi_ref, g_ref, c_ref, x_ref, bonus_ref, gate_ref, gw_ref, gb_ref, wo_ref,
                   o_ref, h_ref, y_ref, *, nc):
    L = CHUNK
    lane_lo = lax.broadcasted_iota(jnp.int32, (L, LANES), 1) < HEAD_DIM

    @pl.when(pl.program_id(1) == 0)
    def _():
        h_ref[...] = jnp.zeros_like(h_ref)

    def chunk(c, hs):
        sl = pl.ds(pl.multiple_of(c * L, L), L)
        out = []
        for p in range(N_PAIRS):
            lanes = slice(p * LANES, (p + 1) * LANES)
            lhs = jnp.concatenate([qh_ref[0, sl, lanes], _stack(g_ref[0, p, c], lane_lo)], axis=0)
            prod = _mm(lhs, _bf(hs[p]))
            y_ref[sl, lanes] = prod[:L] + yi_ref[0, sl, lanes]
            out.append(prod[L:] + _stack(c_ref[0, p, c], lane_lo))
        return tuple(out)

    hs = lax.fori_loop(0, nc, chunk, tuple(h_ref[p] for p in range(N_PAIRS)))
    for p in range(N_PAIRS):
        h_ref[p] = hs[p]

    ones = _head_ones()
    y = y_ref[...]
    mean = _head_sum(y, ones) * (1.0 / HEAD_DIM)
    d = y - mean
    var = _head_sum(d * d, ones) * (1.0 / HEAD_DIM)
    yn = d * lax.rsqrt(var + GN_EPS) * gw_ref[...] + gb_ref[...]
    z = _bf((yn + bonus_ref[0].astype(F32)) * gate_ref[0].astype(F32))
    o_ref[0] = x_ref[0] + _mm(z, wo_ref[...])


def _rwkv_out(qh, yi, g, c, x, bonus, gate, gn_w, gn_b, w_out, nc=8):
    B, T, D = qh.shape
    L = CHUNK
    tt = nc * L
    big = pl.BlockSpec((1, tt, D), lambda b, i: (b, i, 0))
    op = pl.BlockSpec((1, N_PAIRS, nc, L, LANES), lambda b, i: (b, 0, i, 0, 0))
    ins = [qh, yi, g, c, x, bonus, gate, gn_w.reshape(1, D), gn_b.reshape(1, D), w_out.astype(BF16)]
    return pl.pallas_call(
        functools.partial(_rwkv_out_body, nc=nc), grid=(B, T // tt),
        in_specs=[big, big, op, op, big, big, big] + [_full(a.shape) for a in ins[7:]],
        out_specs=big, out_shape=jax.ShapeDtypeStruct((B, T, D), F32),
        scratch_shapes=[pltpu.VMEM((N_PAIRS, 2 * HEAD_DIM, LANES), F32), pltpu.VMEM((tt, D), F32)],
        compiler_params=_params("parallel", "arbitrary"), name="rwkv_out")(*ins)


def _head_norm_rope_t(xt, gain, cos, sin):
    heads = []
    for h in range(N_HEADS):
        xh = xt[h * HEAD_DIM:(h + 1) * HEAD_DIM, :]
        xh = xh * lax.rsqrt(jnp.mean(xh * xh, axis=0, keepdims=True) + RMS_EPS) * gain
        x1, x2 = xh[:ROPE_HALF], xh[ROPE_HALF:ROPE_DIM]
        heads += [x1 * cos - x2 * sin, x2 * cos + x1 * sin, xh[ROPE_DIM:]]
    return jnp.concatenate(heads, axis=0)


def _moba_pre_body(x_ref, nkv_ref, nb_ref, wkv_ref, wq_ref, wg_ref, kn_ref, qn_ref, cos_ref, sin_ref,
                   k_ref, km_ref, qt_ref, vt_ref, g_ref):
    x = x_ref[0]
    xn = x * lax.rsqrt(jnp.mean(x * x, axis=-1, keepdims=True) + RMS_EPS)
    h_kv = (xn * nkv_ref[...]).astype(BF16)
    h_b = (xn * nb_ref[...]).astype(BF16)
    cos, sin = cos_ref[...], sin_ref[...]
    tile = lambda g: jnp.concatenate([g] * (x.shape[0] // LANES), axis=1)
    kvt = _mm(wkv_ref[...], h_kv, NT)
    vt_ref[0, 0] = kvt[D_MODEL:].astype(BF16)
    kt = _head_norm_rope_t(kvt[:D_MODEL], tile(kn_ref[...]), cos, sin)
    tag = lax.broadcasted_iota(jnp.int32, (TAG_LANES, x.shape[0]), 0) == pl.program_id(1)
    tag = jnp.where(tag, 1.0, 0.0)
    for h in range(N_HEADS):
        k_aug = jnp.concatenate([kt[h * HEAD_DIM:(h + 1) * HEAD_DIM], tag], axis=0).T
        k_ref[0, h] = k_aug.astype(BF16)
        km_ref[0, 0, h:h + 1, :] = jnp.mean(k_aug, axis=0, keepdims=True)
    qt = _head_norm_rope_t(_mm(wq_ref[...], h_b, NT), tile(qn_ref[...]), cos, sin)
    qt_ref[0, 0] = (qt * (LOG2E / HEAD_DIM ** 0.5)).astype(BF16)
    gate = _mm(wg_ref[...], h_b, NT)
    g_ref[0, 0] = _bf(gate * _sigmoid(gate))


def _rope_tables_t(T):
    inv_freq = jnp.power(jnp.float32(ROPE_THETA), -jnp.arange(0, ROPE_DIM, 2, dtype=F32) / ROPE_DIM)
    ang = inv_freq[:, None] * jnp.arange(T, dtype=F32)[None, :]
    return jnp.cos(ang), jnp.sin(ang)


def _moba_pre(x, norm_kv, norm_b, w_kv, w_in, k_norm, q_norm):
    B, T, D = x.shape
    tm = MOBA_BLOCK
    nb = T // tm
    cos, sin = _rope_tables_t(T)
    bcast = lambda g: jnp.broadcast_to(g.reshape(HEAD_DIM, 1), (HEAD_DIM, LANES))
    ins = [x, norm_kv.reshape(1, D), norm_b.reshape(1, D), w_kv.T.astype(BF16), w_in[:, :D].T.astype(BF16),
           w_in[:, D:].T.astype(BF16), bcast(k_norm), bcast(q_norm), cos, sin]
    big = pl.BlockSpec((1, tm, D), lambda b, i: (b, i, 0))
    tab = pl.BlockSpec((ROPE_HALF, tm), lambda b, i: (0, i))
    tr = pl.BlockSpec((1, 1, D, tm), lambda b, i: (b, i, 0, 0))
    in_specs = [big] + [_full(a.shape) for a in ins[1:8]] + [tab, tab]
    assert tm == MOBA_BLOCK and nb <= TAG_LANES
    out_specs = [pl.BlockSpec((1, N_HEADS, tm, LANES), lambda b, i: (b, 0, i, 0)),
                 pl.BlockSpec((1, 1, N_HEADS, LANES), lambda b, i: (b, i, 0, 0)), tr, tr, tr]
    out_shape = [jax.ShapeDtypeStruct((B, N_HEADS, T, LANES), BF16), jax.ShapeDtypeStruct((B, nb, N_HEADS, LANES), F32),
                 jax.ShapeDtypeStruct((B, nb, D, tm), BF16), jax.ShapeDtypeStruct((B, nb, D, tm), BF16),
                 jax.ShapeDtypeStruct((B, nb, D, tm), BF16)]
    return pl.pallas_call(
        _moba_pre_body, grid=(B, nb), in_specs=in_specs, out_specs=out_specs, out_shape=out_shape,
        compiler_params=_params("parallel", "parallel"), name="moba_pre")(*ins)


def _attn_schedule(nb):
    items = []
    for d in range(1, nb):
        diag = [(j, j + d) for j in range(nb - d)]
        diag += [(0, nb)] * (-len(diag) % ATTN_GROUP)
        items += diag
    n_trips = -(-len(items) // (ATTN_GROUP * ATTN_STAGES))
    items += [(0, nb)] * (ATTN_GROUP * (n_trips * ATTN_STAGES + 1) - len(items))
    return [v for it in items for v in it], n_trips


def _moba_attn_body(tab_ref, k_ref, km_ref, qt_ref, vt_ref, o_ref, qa_ref, acc_ref, m_ref, sa_ref, sb_ref,
                    *, nb, n_trips):
    BS = MOBA_BLOCK
    G = ATTN_GROUP
    heads = range(2)
    ones = jnp.ones((SUM_ROWS, BS), BF16)
    zeros_tag = jnp.zeros((TAG_LANES, BS), BF16)
    rows = lambda h: slice(h * HEAD_DIM, (h + 1) * HEAD_DIM)
    v_aug = lambda j, h: jnp.concatenate([vt_ref[0, j, rows(h), :], ones], axis=0)
    k_blk = lambda j, h: k_ref[0, h, pl.ds(pl.multiple_of(j * BS, BS), BS), :]

    blk = lax.broadcasted_iota(jnp.int32, (nb, BS), 0)
    blk_f = blk.astype(F32)
    km = [_split(km_ref[0, h]) for h in heads]
    pad = [jnp.full((TAG_LANES - nb, BS), NEG, F32)] if nb < TAG_LANES else []

    def gate(g, carry):
        ih = [(g * GATE_GROUP + e, h) for e in range(GATE_GROUP) for h in heads]
        q = [qt_ref[0, i, rows(h), :] for i, h in ih]
        qz = [jnp.concatenate([x, zeros_tag], axis=0) for x in q]
        gs = [_mm(km[h][0], x) + _mm(km[h][1], x) for x, (i, h) in zip(qz, ih)]
        work = [jnp.where(blk < i, x, -jnp.inf) for x, (i, h) in zip(gs, ih)]
        f = [jnp.full((nb, BS), NEG, F32) for _ in ih]
        for _ in range(MOBA_TOPK):
            mx = [jnp.max(x, axis=0, keepdims=True) for x in work]
            first = [jnp.min(jnp.where(w == m, blk_f, float(nb)), axis=0, keepdims=True) for w, m in zip(work, mx)]
            hit = [blk_f == x for x in first]
            f = [jnp.where(h_, jnp.where(m > -jnp.inf, 0.0, f_), f_) for h_, m, f_ in zip(hit, mx, f)]
            work = [jnp.where(h_, -jnp.inf, w) for h_, w in zip(hit, work)]
        for x, f_, (i, h) in zip(q, f, ih):
            qa_ref[i, h] = jnp.concatenate([x, _bf(jnp.concatenate([f_] + pad, axis=0))], axis=0)
        return carry

    lax.fori_loop(0, nb // GATE_GROUP, gate, 0)
    for h in heads:
        qa_ref[nb, h] = jnp.concatenate([jnp.zeros((HEAD_DIM, BS), BF16), jnp.full((TAG_LANES, BS), NEG, BF16)], axis=0)
    acc_ref[...] = jnp.zeros_like(acc_ref)
    m_ref[...] = jnp.full(m_ref.shape, 0.5 * NEG, F32)

    key_pos = lax.broadcasted_iota(jnp.int32, (BS, BS), 0)
    qry_pos = lax.broadcasted_iota(jnp.int32, (BS, BS), 1)
    eh = [(e, h) for e in range(G) for h in heads]

    def scores(item, causal, g, dst_ref):
        for c, (e, h) in enumerate(eh):
            j, i = item(g, e)
            q_op = jnp.concatenate([qt_ref[0, i, rows(h), :], zeros_tag], axis=0) if causal else qa_ref[i, h]
            dst_ref[c] = _mm(k_blk(j, h), q_op)

    def consume(item, causal, src_ref, g):
        for c, (e, h) in enumerate(eh):
            j, i = item(g, e)
            s = jnp.where(key_pos <= qry_pos, src_ref[c], NEG) if causal else src_ref[c]
            m_old = m_ref[i, h]
            m_new = jnp.maximum(m_old, jnp.max(s, axis=0, keepdims=True))
            acc = jnp.exp2(m_old - m_new) * acc_ref[i, h] + _mm(v_aug(j, h), _bf(jnp.exp2(s - m_new)))
            m_ref[i, h] = m_new
            acc_ref[i, h] = acc

    def stream(item, causal, n_trips, stages):
        scores(item, causal, 0, sa_ref)

        def trip(t, carry):
            g0 = t * stages
            for u in range(0, stages, 2):
                scores(item, causal, g0 + u + 1, sb_ref)
                consume(item, causal, sa_ref, g0 + u)
                scores(item, causal, g0 + u + 2, sa_ref)
                consume(item, causal, sb_ref, g0 + u + 1)
            return carry

        lax.fori_loop(0, n_trips, trip, 0)

    own_item = lambda g, e: (jnp.minimum(g * G + e, nb - 1),) * 2
    stream(own_item, True, nb // (2 * G), 2)
    stream(lambda g, e: (tab_ref[2 * (g * G + e)], tab_ref[2 * (g * G + e) + 1]), False, n_trips, ATTN_STAGES)

    def finish(g, carry):
        for e in range(G):
            i = g * G + e
            acc = [acc_ref[i, h] for h in heads]
            o_ref[0, i] = _bf(jnp.concatenate([a[:HEAD_DIM] / a[HEAD_DIM:HEAD_DIM + 1] for a in acc], axis=0))
        return carry

    lax.fori_loop(0, nb // G, finish, 0)


def _moba_attn(k, kmean, qt, vt):
    B, H, T, _ = k.shape
    BS = MOBA_BLOCK
    nb = T // BS
    assert nb % GATE_GROUP == 0 and nb % (2 * ATTN_GROUP) == 0 and ATTN_STAGES % 2 == 0
    table, n_trips = _attn_schedule(nb)
    in_specs = [pl.BlockSpec(memory_space=pltpu.SMEM),
                pl.BlockSpec((1, 2, T, LANES), lambda b, p: (b, p, 0, 0)),
                pl.BlockSpec((1, 2, nb, LANES), lambda b, p: (b, p, 0, 0)),
                pl.BlockSpec((1, nb, LANES, BS), lambda b, p: (b, 0, p, 0)),
                pl.BlockSpec((1, nb, LANES, BS), lambda b, p: (b, 0, p, 0))]
    chains = 2 * ATTN_GROUP
    scratch = [pltpu.VMEM((nb + 1, 2, LANES, BS), BF16),
               pltpu.VMEM((nb + 1, 2, HEAD_DIM + SUM_ROWS, BS), F32),
               pltpu.VMEM((nb + 1, 2, 1, BS), F32),
               pltpu.VMEM((chains, BS, BS), F32), pltpu.VMEM((chains, BS, BS), F32)]
    return pl.pallas_call(
        functools.partial(_moba_attn_body, nb=nb, n_trips=n_trips), grid=(B, N_PAIRS), in_specs=in_specs,
        out_specs=pl.BlockSpec((1, nb, LANES, BS), lambda b, p: (b, 0, p, 0)),
        out_shape=jax.ShapeDtypeStruct((B, nb, H * HEAD_DIM, BS), BF16), scratch_shapes=scratch,
        compiler_params=_params("parallel", "parallel"), name="moba_attn")(
            jnp.asarray(table, jnp.int32), k, kmean, qt, vt)


def _moba_post_body(x_ref, ot_ref, gt_ref, wo_ref, out_ref):
    BS = MOBA_BLOCK
    for u in range(ot_ref.shape[1]):
        zt = _bf(ot_ref[0, u].astype(F32) * gt_ref[0, u].astype(F32))
        out_ref[0, u * BS:(u + 1) * BS, :] = x_ref[0, u * BS:(u + 1) * BS, :] + _mm(zt, wo_ref[...], TN)


def _moba_post(x, ot, gt, w_out, tm=512):
    B, T, D = x.shape
    big = pl.BlockSpec((1, tm, D), lambda b, i: (b, i, 0))
    tr = pl.BlockSpec((1, tm // MOBA_BLOCK, D, MOBA_BLOCK), lambda b, i: (b, i, 0, 0))
    return pl.pallas_call(
        _moba_post_body, grid=(B, T // tm), in_specs=[big, tr, tr, _full((D, D))], out_specs=big,
        out_shape=jax.ShapeDtypeStruct((B, T, D), F32),
        compiler_params=_params("parallel", "parallel"), name="moba_post")(x, ot, gt, w_out.astype(BF16))


def _rwkv_layer(x, norm_a, mu, w_in, w0, w2, a0, a2, kkw, kaw, rk, gn_w, gn_b, w_out):
    r, ld, k, v, kk, nb, bonus, g = _rwkv_pre(x, norm_a, mu, w_in, w0, w2, a0, a2, kkw, kaw, rk.reshape(-1))
    qh, yi, gm, cm = _rwkv_chunks(r, ld, k, v, kk, nb)
    return _rwkv_out(qh, yi, gm, cm, x, bonus, g, gn_w, gn_b, w_out)


def _moba_layer(x, norm_kv, w_kv, k_norm, norm_b, w_in, q_norm, w_out):
    B, T, D = x.shape
    k, kmean, qt, vt, g = _moba_pre(x, norm_kv, norm_b, w_kv, w_in, k_norm, q_norm)
    o = _moba_attn(k, kmean.transpose(0, 2, 1, 3), qt, vt)
    return _moba_post(x, o, g, w_out)


def kernel(x, norm_a, mu_a, w_in_a, w0_a, w2_a, a0_a, a2_a, kk_a, ka_a, rk_a, gn_w_a, gn_b_a, w_out_a, norm_kv, w_kv, k_norm, norm_b, w_in_b, q_norm_b, w_out_b):
    x = _rwkv_layer(x, norm_a[0], mu_a[0], w_in_a[0], w0_a[0], w2_a[0], a0_a[0], a2_a[0], kk_a[0], ka_a[0],
                    rk_a[0], gn_w_a[0], gn_b_a[0], w_out_a[0])
    return _moba_layer(x, norm_kv, w_kv, k_norm, norm_b[0], w_in_b[0], q_norm_b[0], w_out_b[0])
```

```python
import functools

import jax
import jax.numpy as jnp
from jax import lax
from jax.experimental import pallas as pl
from jax.experimental.pallas import tpu as pltpu

F32 = jnp.float32
BF16 = jnp.bfloat16

D_MODEL = 1024
HEAD_DIM = 64
N_HEADS = D_MODEL // HEAD_DIM
LANES = 128
N_PAIRS = D_MODEL // LANES
LORA = 64
MOBA_BLOCK = 256
MOBA_TOPK = 3
ROPE_THETA = 500000.0
ROPE_DIM = HEAD_DIM // 4
ROPE_HALF = ROPE_DIM // 2
RMS_EPS = 1e-6
GN_EPS = 64e-5
NEG = -1e30
LOG2E = 1.4426950408889634
DECAY_SCALE = 0.6065306597126334
SUM_ROWS = 16
TAG_LANES = LANES - HEAD_DIM
ATTN_GROUP = 2
ATTN_STAGES = 4
GATE_GROUP = 4
CHUNK = 64
VMEM_LIMIT = 56 * 1024 * 1024

NN = (((1,), (0,)), ((), ()))
NT = (((1,), (1,)), ((), ()))
TN = (((0,), (0,)), ((), ()))


def _mm(a, b, dims=NN):
    return lax.dot_general(a, b, dims, preferred_element_type=F32)


def _split(x):
    hi = x.astype(BF16)
    lo = (x - hi.astype(F32)).astype(BF16)
    return hi, lo


def _bf(x):
    return x.astype(BF16)


def _head_ones():
    r = lax.broadcasted_iota(jnp.int32, (LANES, LANES), 0) // HEAD_DIM
    c = lax.broadcasted_iota(jnp.int32, (LANES, LANES), 1) // HEAD_DIM
    return jnp.where(r == c, 1.0, 0.0).astype(BF16)


def _head_sum(x, ones):
    parts = [_mm(_bf(x[:, p * LANES:(p + 1) * LANES]), ones) for p in range(x.shape[1] // LANES)]
    return jnp.concatenate(parts, axis=1)


def _sigmoid(x):
    return 1.0 / (1.0 + jnp.exp(-x))


def _rms(x, g):
    return x * lax.rsqrt(jnp.mean(x * x, axis=-1, keepdims=True) + RMS_EPS) * g


def _params(*sem):
    return pltpu.CompilerParams(dimension_semantics=sem, vmem_limit_bytes=VMEM_LIMIT)


def _full(shape):
    return pl.BlockSpec(shape, lambda *_: (0,) * len(shape))


def _rwkv_pre_body(x_ref, xp_ref, na_ref, mu_ref, wr_ref, wwd_ref, wk_ref, wv_ref, wad_ref, wg_ref,
                   w0_ref, w2_ref, a0_ref, a2_ref, kkw_ref, kaw_ref, rk_ref,
                   r_ref, ld_ref, k_ref, v_ref, kk_ref, nb_ref, bonus_ref, g_ref):
    x = x_ref[0]
    gn = na_ref[...]
    h = _rms(x, gn)
    hp = _rms(xp_ref[0], gn)
    prev = jnp.where(pl.program_id(1) == 0, 0.0, hp[7:8, :])
    rolled = pltpu.roll(h, 1, axis=0)
    first = lax.broadcasted_iota(jnp.int32, h.shape, 0) == 0
    xx = jnp.where(first, prev, rolled) - h

    def mix(i):
        return (h + xx * mu_ref[i:i + 1, :]).astype(BF16)

    r = _mm(mix(0), wr_ref[...])
    wdn = _mm(mix(1), wwd_ref[...])
    k = _mm(mix(2), wk_ref[...])
    v = _mm(mix(3), wv_ref[...])
    adn = _mm(mix(4), wad_ref[...])
    gp = _mm(mix(5), wg_ref[...])

    ld_ref[0] = -DECAY_SCALE * _sigmoid(w0_ref[...] + _mm(jnp.tanh(wdn).astype(BF16), w2_ref[...]))
    a = _sigmoid(a0_ref[...] + _mm(adn.astype(BF16), a2_ref[...]))

    ones = _head_ones()
    kk = k * kkw_ref[...]
    kk = kk * lax.rsqrt(jnp.maximum(_head_sum(kk * kk, ones), 1e-24))
    k2 = k * (1.0 + (a - 1.0) * kaw_ref[...])
    r_ref[0] = r
    k_ref[0] = k2
    v_ref[0] = v
    kk_ref[0] = kk
    nb_ref[0] = -(kk * a)
    bonus_ref[0] = _bf(_head_sum(r * k2 * rk_ref[...], ones) * v)
    g_ref[0] = _bf(gp * _sigmoid(gp))


def _rwkv_pre(x, norm_a, mu, w_in, w0, w2, a0, a2, kkw, kaw, rk, tm=256):
    B, T, D = x.shape
    offs = [0, D, D + LORA, 2 * D + LORA, 3 * D + LORA, 3 * D + 2 * LORA, 4 * D + 2 * LORA]
    ws = [w_in[:, offs[i]:offs[i + 1]].astype(BF16) for i in range(6)]
    row = lambda a: a.reshape(1, D)
    big = pl.BlockSpec((1, tm, D), lambda b, i: (b, i, 0))
    prev = pl.BlockSpec((1, 8, D), lambda b, i: (b, jnp.maximum(i * (tm // 8) - 1, 0), 0))
    ins = [x, x, row(norm_a), mu] + ws + [row(w0), w2.astype(BF16), row(a0), a2.astype(BF16),
                                          row(kkw), row(kaw), row(rk)]
    in_specs = [big, prev] + [_full(a.shape) for a in ins[2:]]
    out = lambda dt: jax.ShapeDtypeStruct((B, T, D), dt)
    return pl.pallas_call(
        _rwkv_pre_body, grid=(B, T // tm), in_specs=in_specs, out_specs=[big] * 8,
        out_shape=[out(F32)] * 6 + [out(BF16)] * 2,
        compiler_params=_params("parallel", "parallel"), name="rwkv_pre")(*ins)


def _stack(x, lane_lo):
    return jnp.concatenate([jnp.where(lane_lo, x, 0.0), jnp.where(lane_lo, 0.0, x)], axis=0)


def _unstack(z):
    half = z.shape[0] // 2
    return z[:half] + z[half:]


def _rwkv_chunk_body(r_ref, ld_ref, k_ref, v_ref, kk_ref, nb_ref, qh_ref, yi_ref, g_ref, c_ref, *, nc):
    L = CHUNK
    S = 2 * L
    lane_lo = lax.broadcasted_iota(jnp.int32, (L, LANES), 1) < HEAD_DIM
    row = lax.broadcasted_iota(jnp.int32, (L, LANES), 0)
    pi = lax.broadcasted_iota(jnp.int32, (2 * S, 2 * S), 0)
    pj = lax.broadcasted_iota(jnp.int32, (2 * S, 2 * S), 1)
    same_head = ((pi >> 6) & 1) == ((pj >> 6) & 1)
    t_i, s_j = pi & (L - 1), pj & (L - 1)
    valid = same_head & ((s_j < t_i) | ((pi >= S) & (s_j == t_i)))
    ei = lax.broadcasted_iota(jnp.int32, (S, S), 0)
    ej = lax.broadcasted_iota(jnp.int32, (S, S), 1)
    eye = ei == ej
    stack = lambda x: _bf(_stack(x, lane_lo))

    cs = range(nc)
    each = lambda f, *xs: [f(*a) for a in zip(*xs)]
    load = lambda ref: [ref[0, c * L:(c + 1) * L, :] for c in cs]
    r, ld, k, v, kk, nb = (load(ref) for ref in (r_ref, ld_ref, k_ref, v_ref, kk_ref, nb_ref))

    def cumsum(x):
        d = 1
        while d < L:
            x = x + jnp.where(row >= d, pltpu.roll(x, d, axis=0), 0.0)
            d *= 2
        return x

    cum = each(cumsum, ld)
    cref = [x[L // 2 - 1:L // 2, :] for x in cum]
    clast = [x[L - 1:L, :] for x in cum]
    ecum = each(lambda a, b: a - b, cum, cref)
    e_in = [jnp.exp(x) for x in ecum]
    e_ex = each(lambda a, b: jnp.exp(a - b), ecum, ld)
    e_neg = [jnp.exp(-x) for x in ecum]
    p_ref = [jnp.exp(x) for x in cref]
    e_last = each(lambda a, b: jnp.exp(a - b), clast, cref)
    mul = lambda a, b: a * b
    q_m, q_n = each(mul, kk, e_ex), each(mul, r, e_in)
    k_b, k_k = each(mul, nb, e_neg), each(mul, k, e_neg)
    lhs = each(lambda a, b: jnp.concatenate([stack(a), stack(b)], axis=0), q_m, q_n)
    rhs = each(lambda a, b: jnp.concatenate([stack(a), stack(b)], axis=0), k_b, k_k)
    pm = each(lambda a, b: jnp.where(valid, _mm(a, b, NT), 0.0), lhs, rhs)
    m_b = [x[:S, :S] for x in pm]
    m_k, n_b, n_k = ([_bf(x[rs, cs_]) for x in pm] for rs, cs_ in
                     ((slice(0, S), slice(S, 2 * S)), (slice(S, 2 * S), slice(0, S)), (slice(S, 2 * S), slice(S, 2 * S))))
    tm = [jnp.where(eye, 1.0, 0.0) + x for x in m_b]
    xp = m_b
    for _ in range(5):
        xp = [_mm(xb, xb) for xb in (_bf(x) for x in xp)]
        tm = each(lambda t, x: t + _mm(_bf(t), _bf(x)), tm, xp)
    sv = [stack(x) for x in v]
    mkv = each(lambda a, b: _bf(_mm(a, b)), m_k, sv)
    qm0 = each(lambda a, b: stack(a * b), q_m, p_ref)
    wu = each(lambda t, a, b: _bf(_mm(_bf(t), jnp.concatenate([a, b], axis=1))), tm, qm0, mkv)
    kbl_t = each(lambda a, b: _bf(_stack(a * b, lane_lo).T), k_b, e_last)
    kkl_t = each(lambda a, b: _bf(_stack(a * b, lane_lo).T), k_k, e_last)
    gc = each(_mm, kbl_t, wu)
    kv = each(_mm, kkl_t, sv)
    nw = each(_mm, n_b, wu)
    nv = each(_mm, n_k, sv)
    for c in cs:
        sl = slice(c * L, (c + 1) * L)
        qh_ref[0, sl, :] = _bf(r[c] * (e_in[c] * p_ref[c]) + _unstack(nw[c][:, :S]))
        yi_ref[0, sl, :] = _bf(_unstack(nw[c][:, S:] + nv[c]))
        g_ref[0, 0, c] = _bf(_unstack(gc[c][:, :S] + jnp.where(eye, jnp.exp(clast[c]), 0.0)))
        c_ref[0, 0, c] = _bf(_unstack(gc[c][:, S:] + kv[c]))


def _rwkv_chunks(r, ld, k, v, kk, nb, nc=16):
    B, T, D = r.shape
    L = CHUNK
    tt = nc * L
    nch = T // L
    nat = pl.BlockSpec((1, tt, LANES), lambda b, p, i: (b, i, p))
    op = pl.BlockSpec((1, 1, nc, L, LANES), lambda b, p, i: (b, p, i, 0, 0))
    nat_shape = lambda dt: jax.ShapeDtypeStruct((B, T, D), dt)
    op_shape = lambda dt: jax.ShapeDtypeStruct((B, N_PAIRS, nch, L, LANES), dt)
    return pl.pallas_call(
        functools.partial(_rwkv_chunk_body, nc=nc), grid=(B, N_PAIRS, T // tt),
        in_specs=[nat] * 6, out_specs=[nat, nat, op, op],
        out_shape=[nat_shape(BF16), nat_shape(BF16), op_shape(BF16), op_shape(BF16)],
        compiler_params=_params("parallel", "parallel", "parallel"), name="rwkv_chunks")(r, ld, k, v, kk, nb)


def _rwkv_out_body(qh_ref, yi_ref, g_ref, c_ref, x_ref, bonus_ref, gate_ref, gw_ref, gb_ref, wo_ref,
                   o_ref, h_ref, y_ref, *, nc):
    L = CHUNK
    lane_lo = lax.broadcasted_iota(jnp.int32, (L, LANES), 1) < HEAD_DIM

    @pl.when(pl.program_id(1) == 0)
    def _():
        h_ref[...] = jnp.zeros_like(h_ref)

    def chunk(c, hs):
        sl = pl.ds(pl.multiple_of(c * L, L), L)
        out = []
        for p in range(N_PAIRS):
            lanes = slice(p * LANES, (p + 1) * LANES)
            lhs = jnp.concatenate([qh_ref[0, sl, lanes], _stack(g_ref[0, p, c], lane_lo)], axis=0)
            prod = _mm(lhs, _bf(hs[p]))
            y_ref[sl, lanes] = prod[:L] + yi_ref[0, sl, lanes].astype(F32)
            out.append(prod[L:] + _stack(c_ref[0, p, c], lane_lo).astype(F32))
        return tuple(out)

    hs = lax.fori_loop(0, nc, chunk, tuple(h_ref[p] for p in range(N_PAIRS)))
    for p in range(N_PAIRS):
        h_ref[p] = hs[p]

    ones = _head_ones()
    y = y_ref[...]
    mean = _head_sum(y, ones) * (1.0 / HEAD_DIM)
    d = y - mean
    var = _head_sum(d * d, ones) * (1.0 / HEAD_DIM)
    yn = d * lax.rsqrt(var + GN_EPS) * gw_ref[...] + gb_ref[...]
    z = _bf((yn + bonus_ref[0].astype(F32)) * gate_ref[0].astype(F32))
    o_ref[0] = x_ref[0] + _mm(z, wo_ref[...])


def _rwkv_out(qh, yi, g, c, x, bonus, gate, gn_w, gn_b, w_out, nc=8):
    B, T, D = qh.shape
    L = CHUNK
    tt = nc * L
    big = pl.BlockSpec((1, tt, D), lambda b, i: (b, i, 0))
    op = pl.BlockSpec((1, N_PAIRS, nc, L, LANES), lambda b, i: (b, 0, i, 0, 0))
    ins = [qh, yi, g, c, x, bonus, gate, gn_w.reshape(1, D), gn_b.reshape(1, D), w_out.astype(BF16)]
    return pl.pallas_call(
        functools.partial(_rwkv_out_body, nc=nc), grid=(B, T // tt),
        in_specs=[big, big, op, op, big, big, big] + [_full(a.shape) for a in ins[7:]],
        out_specs=big, out_shape=jax.ShapeDtypeStruct((B, T, D), F32),
        scratch_shapes=[pltpu.VMEM((N_PAIRS, 2 * HEAD_DIM, LANES), F32), pltpu.VMEM((tt, D), F32)],
        compiler_params=_params("parallel", "arbitrary"), name="rwkv_out")(*ins)


def _head_norm_rope_t(xt, gain, cos, sin):
    heads = []
    for h in range(N_HEADS):
        xh = xt[h * HEAD_DIM:(h + 1) * HEAD_DIM, :]
        xh = xh * lax.rsqrt(jnp.mean(xh * xh, axis=0, keepdims=True) + RMS_EPS) * gain
        x1, x2 = xh[:ROPE_HALF], xh[ROPE_HALF:ROPE_DIM]
        heads += [x1 * cos - x2 * sin, x2 * cos + x1 * sin, xh[ROPE_DIM:]]
    return jnp.concatenate(heads, axis=0)


def _moba_pre_body(x_ref, nkv_ref, nb_ref, wkv_ref, wq_ref, wg_ref, kn_ref, qn_ref, cos_ref, sin_ref,
                   k_ref, km_ref, qt_ref, vt_ref, g_ref):
    x = x_ref[0]
    xn = x * lax.rsqrt(jnp.mean(x * x, axis=-1, keepdims=True) + RMS_EPS)
    h_kv = (xn * nkv_ref[...]).astype(BF16)
    h_b = (xn * nb_ref[...]).astype(BF16)
    cos, sin = cos_ref[...], sin_ref[...]
    tile = lambda g: jnp.concatenate([g] * (x.shape[0] // LANES), axis=1)
    kt = _head_norm_rope_t(_mm(wkv_ref[:D_MODEL, :], h_kv, NT), tile(kn_ref[...]), cos, sin)
    tag = lax.broadcasted_iota(jnp.int32, (TAG_LANES, x.shape[0]), 0) == pl.program_id(1)
    tag = jnp.where(tag, 1.0, 0.0)
    for h in range(N_HEADS):
        k_aug = jnp.concatenate([kt[h * HEAD_DIM:(h + 1) * HEAD_DIM], tag], axis=0).T
        k_ref[0, h] = k_aug.astype(BF16)
        km_ref[0, 0, h:h + 1, :] = jnp.mean(k_aug, axis=0, keepdims=True)
    qt = _head_norm_rope_t(_mm(wq_ref[...], h_b, NT), tile(qn_ref[...]), cos, sin)
    qt_ref[0, 0] = (qt * (LOG2E / HEAD_DIM ** 0.5)).astype(BF16)
    gate = _mm(wg_ref[...], h_b, NT)
    g_ref[0, 0] = _bf(gate * _sigmoid(gate))
    vt_ref[0, 0] = _bf(_mm(wkv_ref[D_MODEL:, :], h_kv, NT))


def _rope_tables_t(T):
    inv_freq = jnp.power(jnp.float32(ROPE_THETA), -jnp.arange(0, ROPE_DIM, 2, dtype=F32) / ROPE_DIM)
    ang = inv_freq[:, None] * jnp.arange(T, dtype=F32)[None, :]
    return jnp.cos(ang), jnp.sin(ang)


def _moba_pre(x, norm_kv, norm_b, w_kv, w_in, k_norm, q_norm):
    B, T, D = x.shape
    tm = MOBA_BLOCK
    nb = T // tm
    cos, sin = _rope_tables_t(T)
    bcast = lambda g: jnp.broadcast_to(g.reshape(HEAD_DIM, 1), (HEAD_DIM, LANES))
    ins = [x, norm_kv.reshape(1, D), norm_b.reshape(1, D), w_kv.T.astype(BF16), w_in[:, :D].T.astype(BF16),
           w_in[:, D:].T.astype(BF16), bcast(k_norm), bcast(q_norm), cos, sin]
    big = pl.BlockSpec((1, tm, D), lambda b, i: (b, i, 0))
    tab = pl.BlockSpec((ROPE_HALF, tm), lambda b, i: (0, i))
    tr = pl.BlockSpec((1, 1, D, tm), lambda b, i: (b, i, 0, 0))
    in_specs = [big] + [_full(a.shape) for a in ins[1:8]] + [tab, tab]
    assert tm == MOBA_BLOCK and nb <= TAG_LANES
    out_specs = [pl.BlockSpec((1, N_HEADS, tm, LANES), lambda b, i: (b, 0, i, 0)),
                 pl.BlockSpec((1, 1, N_HEADS, LANES), lambda b, i: (b, i, 0, 0)), tr, tr, tr]
    out_shape = [jax.ShapeDtypeStruct((B, N_HEADS, T, LANES), BF16), jax.ShapeDtypeStruct((B, nb, N_HEADS, LANES), F32),
                 jax.ShapeDtypeStruct((B, nb, D, tm), BF16), jax.ShapeDtypeStruct((B, nb, D, tm), BF16),
                 jax.ShapeDtypeStruct((B, nb, D, tm), BF16)]
    return pl.pallas_call(
        _moba_pre_body, grid=(B, nb), in_specs=in_specs, out_specs=out_specs, out_shape=out_shape,
        compiler_params=_params("parallel", "parallel"), name="moba_pre")(*ins)


def _attn_schedule(nb):
    items = []
    for d in range(1, nb):
        diag = [(j, j + d) for j in range(nb - d)]
        diag += [(0, nb)] * (-len(diag) % ATTN_GROUP)
        items += diag
    n_trips = -(-len(items) // (ATTN_GROUP * ATTN_STAGES))
    items += [(0, nb)] * (ATTN_GROUP * (n_trips * ATTN_STAGES + 1) - len(items))
    return [v for it in items for v in it], n_trips


def _moba_attn_body(tab_ref, k_ref, km_ref, qt_ref, vt_ref, o_ref, qa_ref, acc_ref, m_ref, sa_ref, sb_ref,
                    *, nb, n_trips):
    BS = MOBA_BLOCK
    G = ATTN_GROUP
    heads = range(2)
    ones = jnp.ones((SUM_ROWS, BS), BF16)
    zeros_tag = jnp.zeros((TAG_LANES, BS), BF16)
    rows = lambda h: slice(h * HEAD_DIM, (h + 1) * HEAD_DIM)
    v_aug = lambda j, h: jnp.concatenate([vt_ref[0, j, rows(h), :], ones], axis=0)
    k_blk = lambda j, h: k_ref[0, h, pl.ds(pl.multiple_of(j * BS, BS), BS), :]

    blk = lax.broadcasted_iota(jnp.int32, (nb, BS), 0)
    blk_f = blk.astype(F32)
    km = [_split(km_ref[0, h]) for h in heads]
    pad = [jnp.full((TAG_LANES - nb, BS), NEG, F32)] if nb < TAG_LANES else []

    def gate(g, carry):
        ih = [(g * GATE_GROUP + e, h) for e in range(GATE_GROUP) for h in heads]
        q = [qt_ref[0, i, rows(h), :] for i, h in ih]
        qz = [jnp.concatenate([x, zeros_tag], axis=0) for x in q]
        gs = [_mm(km[h][0], x) + _mm(km[h][1], x) for x, (i, h) in zip(qz, ih)]
        work = [jnp.where(blk < i, x, -jnp.inf) for x, (i, h) in zip(gs, ih)]
        f = [jnp.full((nb, BS), NEG, F32) for _ in ih]
        for _ in range(MOBA_TOPK):
            mx = [jnp.max(x, axis=0, keepdims=True) for x in work]
            first = [jnp.min(jnp.where(w == m, blk_f, float(nb)), axis=0, keepdims=True) for w, m in zip(work, mx)]
            hit = [blk_f == x for x in first]
            f = [jnp.where(h_, jnp.where(m > -jnp.inf, 0.0, f_), f_) for h_, m, f_ in zip(hit, mx, f)]
            work = [jnp.where(h_, -jnp.inf, w) for h_, w in zip(hit, work)]
        for x, f_, (i, h) in zip(q, f, ih):
            qa_ref[i, h] = jnp.concatenate([x, _bf(jnp.concatenate([f_] + pad, axis=0))], axis=0)
        return carry

    lax.fori_loop(0, nb // GATE_GROUP, gate, 0)
    for h in heads:
        qa_ref[nb, h] = jnp.concatenate([jnp.zeros((HEAD_DIM, BS), BF16), jnp.full((TAG_LANES, BS), NEG, BF16)], axis=0)
    acc_ref[...] = jnp.zeros_like(acc_ref)
    m_ref[...] = jnp.full(m_ref.shape, 0.5 * NEG, F32)

    key_pos = lax.broadcasted_iota(jnp.int32, (BS, BS), 0)
    qry_pos = lax.broadcasted_iota(jnp.int32, (BS, BS), 1)
    eh = [(e, h) for e in range(G) for h in heads]

    def scores(item, causal, g, dst_ref):
        for c, (e, h) in enumerate(eh):
            j, i = item(g, e)
            q_op = jnp.concatenate([qt_ref[0, i, rows(h), :], zeros_tag], axis=0) if causal else qa_ref[i, h]
            dst_ref[c] = _mm(k_blk(j, h), q_op)

    def consume(item, causal, src_ref, g):
        for c, (e, h) in enumerate(eh):
            j, i = item(g, e)
            s = jnp.where(key_pos <= qry_pos, src_ref[c], NEG) if causal else src_ref[c]
            m_old = m_ref[i, h]
            m_new = jnp.maximum(m_old, jnp.max(s, axis=0, keepdims=True))
            acc = jnp.exp2(m_old - m_new) * acc_ref[i, h] + _mm(v_aug(j, h), _bf(jnp.exp2(s - m_new)))
            m_ref[i, h] = m_new
            acc_ref[i, h] = acc

    def stream(item, causal, n_trips, stages):
        scores(item, causal, 0, sa_ref)

        def trip(t, carry):
            g0 = t * stages
            for u in range(0, stages, 2):
                scores(item, causal, g0 + u + 1, sb_ref)
                consume(item, causal, sa_ref, g0 + u)
                scores(item, causal, g0 + u + 2, sa_ref)
                consume(item, causal, sb_ref, g0 + u + 1)
            return carry

        lax.fori_loop(0, n_trips, trip, 0)

    own_item = lambda g, e: (jnp.minimum(g * G + e, nb - 1),) * 2
    stream(own_item, True, nb // (2 * G), 2)
    stream(lambda g, e: (tab_ref[2 * (g * G + e)], tab_ref[2 * (g * G + e) + 1]), False, n_trips, ATTN_STAGES)

    def finish(g, carry):
        for e in range(G):
            i = g * G + e
            acc = [acc_ref[i, h] for h in heads]
            o_ref[0, i] = _bf(jnp.concatenate([a[:HEAD_DIM] / a[HEAD_DIM:HEAD_DIM + 1] for a in acc], axis=0))
        return carry

    lax.fori_loop(0, nb // G, finish, 0)


def _moba_attn(k, kmean, qt, vt):
    B, H, T, _ = k.shape
    BS = MOBA_BLOCK
    nb = T // BS
    assert nb % GATE_GROUP == 0 and nb % (2 * ATTN_GROUP) == 0 and ATTN_STAGES % 2 == 0
    table, n_trips = _attn_schedule(nb)
    in_specs = [pl.BlockSpec(memory_space=pltpu.SMEM),
                pl.BlockSpec((1, 2, T, LANES), lambda b, p: (b, p, 0, 0)),
                pl.BlockSpec((1, 2, nb, LANES), lambda b, p: (b, p, 0, 0)),
                pl.BlockSpec((1, nb, LANES, BS), lambda b, p: (b, 0, p, 0)),
                pl.BlockSpec((1, nb, LANES, BS), lambda b, p: (b, 0, p, 0))]
    chains = 2 * ATTN_GROUP
    scratch = [pltpu.VMEM((nb + 1, 2, LANES, BS), BF16),
               pltpu.VMEM((nb + 1, 2, HEAD_DIM + SUM_ROWS, BS), F32),
               pltpu.VMEM((nb + 1, 2, 1, BS), F32),
               pltpu.VMEM((chains, BS, BS), F32), pltpu.VMEM((chains, BS, BS), F32)]
    return pl.pallas_call(
        functools.partial(_moba_attn_body, nb=nb, n_trips=n_trips), grid=(B, N_PAIRS), in_specs=in_specs,
        out_specs=pl.BlockSpec((1, nb, LANES, BS), lambda b, p: (b, 0, p, 0)),
        out_shape=jax.ShapeDtypeStruct((B, nb, H * HEAD_DIM, BS), BF16), scratch_shapes=scratch,
        compiler_params=_params("parallel", "parallel"), name="moba_attn")(
            jnp.asarray(table, jnp.int32), k, kmean, qt, vt)


def _moba_post_body(x_ref, ot_ref, gt_ref, wo_ref, out_ref):
    BS = MOBA_BLOCK
    for u in range(ot_ref.shape[1]):
        zt = _bf(ot_ref[0, u].astype(F32) * gt_ref[0, u].astype(F32))
        out_ref[0, u * BS:(u + 1) * BS, :] = x_ref[0, u * BS:(u + 1) * BS, :] + _mm(zt, wo_ref[...], TN)


def _moba_post(x, ot, gt, w_out, tm=512):
    B, T, D = x.shape
    big = pl.BlockSpec((1, tm, D), lambda b, i: (b, i, 0))
    tr = pl.BlockSpec((1, tm // MOBA_BLOCK, D, MOBA_BLOCK), lambda b, i: (b, i, 0, 0))
    return pl.pallas_call(
        _moba_post_body, grid=(B, T // tm), in_specs=[big, tr, tr, _full((D, D))], out_specs=big,
        out_shape=jax.ShapeDtypeStruct((B, T, D), F32),
        compiler_params=_params("parallel", "parallel"), name="moba_post")(x, ot, gt, w_out.astype(BF16))


def _rwkv_layer(x, norm_a, mu, w_in, w0, w2, a0, a2, kkw, kaw, rk, gn_w, gn_b, w_out):
    r, ld, k, v, kk, nb, bonus, g = _rwkv_pre(x, norm_a, mu, w_in, w0, w2, a0, a2, kkw, kaw, rk.reshape(-1))
    qh, yi, gm, cm = _rwkv_chunks(r, ld, k, v, kk, nb)
    return _rwkv_out(qh, yi, gm, cm, x, bonus, g, gn_w, gn_b, w_out)


def _moba_layer(x, norm_kv, w_kv, k_norm, norm_b, w_in, q_norm, w_out):
    B, T, D = x.shape
    k, kmean, qt, vt, g = _moba_pre(x, norm_kv, norm_b, w_kv, w_in, k_norm, q_norm)
    o = _moba_attn(k, kmean.transpose(0, 2, 1, 3), qt, vt)
    return _moba_post(x, o, g, w_out)


def kernel(x, norm_a, mu_a, w_in_a, w0_a, w2_a, a0_a, a2_a, kk_a, ka_a, rk_a, gn_w_a, gn_b_a, w_out_a, norm_kv, w_kv, k_norm, norm_b, w_in_b, q_norm_b, w_out_b):
    x = _rwkv_layer(x, norm_a[0], mu_a[0], w_in_a[0], w0_a[0], w2_a[0], a0_a[0], a2_a[0], kk_a[0], ka_a[0],
                    rk_a[0], gn_w_a[0], gn_b_a[0], w_out_a[0])
    return _moba_layer(x, norm_kv, w_kv, k_norm, norm_b[0], w_in_b[0], q_norm_b[0], w_out_b[0])
```

```python
import functools

import jax
import jax.numpy as jnp
from jax import lax
from jax.experimental import pallas as pl
from jax.experimental.pallas import tpu as pltpu

F32 = jnp.float32
BF16 = jnp.bfloat16

D_MODEL = 1024
HEAD_DIM = 64
N_HEADS = D_MODEL // HEAD_DIM
LANES = 128
N_PAIRS = D_MODEL // LANES
LORA = 64
MOBA_BLOCK = 256
MOBA_TOPK = 3
ROPE_THETA = 500000.0
ROPE_DIM = HEAD_DIM // 4
ROPE_HALF = ROPE_DIM // 2
RMS_EPS = 1e-6
GN_EPS = 64e-5
NEG = -1e30
LOG2E = 1.4426950408889634
DECAY_SCALE = 0.6065306597126334
SUM_ROWS = 16
TAG_LANES = LANES - HEAD_DIM
ATTN_GROUP = 2
ATTN_STAGES = 6
GATE_GROUP = 4
CHUNK = 64
VMEM_LIMIT = 56 * 1024 * 1024

NN = (((1,), (0,)), ((), ()))
NT = (((1,), (1,)), ((), ()))
TN = (((0,), (0,)), ((), ()))


def _mm(a, b, dims=NN):
    return lax.dot_general(a, b, dims, preferred_element_type=F32)


def _split(x):
    hi = x.astype(BF16)
    lo = (x - hi.astype(F32)).astype(BF16)
    return hi, lo


def _bf(x):
    return x.astype(BF16)


def _head_ones():
    r = lax.broadcasted_iota(jnp.int32, (LANES, LANES), 0) // HEAD_DIM
    c = lax.broadcasted_iota(jnp.int32, (LANES, LANES), 1) // HEAD_DIM
    return jnp.where(r == c, 1.0, 0.0).astype(BF16)


def _head_sum(x, ones):
    parts = [_mm(_bf(x[:, p * LANES:(p + 1) * LANES]), ones) for p in range(x.shape[1] // LANES)]
    return jnp.concatenate(parts, axis=1)


def _sigmoid(x):
    return 1.0 / (1.0 + jnp.exp(-x))


def _rms(x, g):
    return x * lax.rsqrt(jnp.mean(x * x, axis=-1, keepdims=True) + RMS_EPS) * g


def _params(*sem):
    return pltpu.CompilerParams(dimension_semantics=sem, vmem_limit_bytes=VMEM_LIMIT)


def _full(shape):
    return pl.BlockSpec(shape, lambda *_: (0,) * len(shape))


def _rwkv_pre_body(x_ref, xp_ref, na_ref, mu_ref, wr_ref, wwd_ref, wk_ref, wv_ref, wad_ref, wg_ref,
                   w0_ref, w2_ref, a0_ref, a2_ref, kkw_ref, kaw_ref, rk_ref,
                   r_ref, ld_ref, k_ref, v_ref, kk_ref, nb_ref, bonus_ref, g_ref):
    x = x_ref[0]
    gn = na_ref[...]
    h = _rms(x, gn)
    hp = _rms(xp_ref[0], gn)
    prev = jnp.where(pl.program_id(1) == 0, 0.0, hp[7:8, :])
    rolled = pltpu.roll(h, 1, axis=0)
    first = lax.broadcasted_iota(jnp.int32, h.shape, 0) == 0
    xx = jnp.where(first, prev, rolled) - h

    def mix(i):
        return (h + xx * mu_ref[i:i + 1, :]).astype(BF16)

    r = _mm(mix(0), wr_ref[...])
    wdn = _mm(mix(1), wwd_ref[...])
    k = _mm(mix(2), wk_ref[...])
    v = _mm(mix(3), wv_ref[...])
    adn = _mm(mix(4), wad_ref[...])
    gp = _mm(mix(5), wg_ref[...])

    ld_ref[0] = -DECAY_SCALE * _sigmoid(w0_ref[...] + _mm(jnp.tanh(wdn).astype(BF16), w2_ref[...]))
    a = _sigmoid(a0_ref[...] + _mm(adn.astype(BF16), a2_ref[...]))

    ones = _head_ones()
    kk = k * kkw_ref[...]
    kk = kk * lax.rsqrt(jnp.maximum(_head_sum(kk * kk, ones), 1e-24))
    k2 = k * (1.0 + (a - 1.0) * kaw_ref[...])
    r_ref[0] = r
    k_ref[0] = k2
    v_ref[0] = v
    kk_ref[0] = kk
    nb_ref[0] = -(kk * a)
    bonus_ref[0] = _bf(_head_sum(r * k2 * rk_ref[...], ones) * v)
    g_ref[0] = _bf(gp * _sigmoid(gp))


def _rwkv_pre(x, norm_a, mu, w_in, w0, w2, a0, a2, kkw, kaw, rk, tm=256):
    B, T, D = x.shape
    offs = [0, D, D + LORA, 2 * D + LORA, 3 * D + LORA, 3 * D + 2 * LORA, 4 * D + 2 * LORA]
    ws = [w_in[:, offs[i]:offs[i + 1]].astype(BF16) for i in range(6)]
    row = lambda a: a.reshape(1, D)
    big = pl.BlockSpec((1, tm, D), lambda b, i: (b, i, 0))
    prev = pl.BlockSpec((1, 8, D), lambda b, i: (b, jnp.maximum(i * (tm // 8) - 1, 0), 0))
    ins = [x, x, row(norm_a), mu] + ws + [row(w0), w2.astype(BF16), row(a0), a2.astype(BF16),
                                          row(kkw), row(kaw), row(rk)]
    in_specs = [big, prev] + [_full(a.shape) for a in ins[2:]]
    out = lambda dt: jax.ShapeDtypeStruct((B, T, D), dt)
    return pl.pallas_call(
        _rwkv_pre_body, grid=(B, T // tm), in_specs=in_specs, out_specs=[big] * 8,
        out_shape=[out(F32)] * 6 + [out(BF16)] * 2,
        compiler_params=_params("parallel", "parallel"), name="rwkv_pre")(*ins)


def _stack(x, lane_lo):
    return jnp.concatenate([jnp.where(lane_lo, x, 0.0), jnp.where(lane_lo, 0.0, x)], axis=0)


def _unstack(z):
    half = z.shape[0] // 2
    return z[:half] + z[half:]


def _rwkv_chunk_body(r_ref, ld_ref, k_ref, v_ref, kk_ref, nb_ref, qh_ref, yi_ref, g_ref, c_ref, *, nc):
    L = CHUNK
    S = 2 * L
    lane_lo = lax.broadcasted_iota(jnp.int32, (L, LANES), 1) < HEAD_DIM
    row = lax.broadcasted_iota(jnp.int32, (L, LANES), 0)
    pi = lax.broadcasted_iota(jnp.int32, (2 * S, 2 * S), 0)
    pj = lax.broadcasted_iota(jnp.int32, (2 * S, 2 * S), 1)
    same_head = ((pi >> 6) & 1) == ((pj >> 6) & 1)
    t_i, s_j = pi & (L - 1), pj & (L - 1)
    valid = same_head & ((s_j < t_i) | ((pi >= S) & (s_j == t_i)))
    ei = lax.broadcasted_iota(jnp.int32, (S, S), 0)
    ej = lax.broadcasted_iota(jnp.int32, (S, S), 1)
    eye = ei == ej
    stack = lambda x: _bf(_stack(x, lane_lo))

    cs = range(nc)
    each = lambda f, *xs: [f(*a) for a in zip(*xs)]
    load = lambda ref: [ref[0, c * L:(c + 1) * L, :] for c in cs]
    r, ld, k, v, kk, nb = (load(ref) for ref in (r_ref, ld_ref, k_ref, v_ref, kk_ref, nb_ref))

    def cumsum(x):
        d = 1
        while d < L:
            x = x + jnp.where(row >= d, pltpu.roll(x, d, axis=0), 0.0)
            d *= 2
        return x

    cum = each(cumsum, ld)
    cref = [x[L // 2 - 1:L // 2, :] for x in cum]
    clast = [x[L - 1:L, :] for x in cum]
    ecum = each(lambda a, b: a - b, cum, cref)
    e_in = [jnp.exp(x) for x in ecum]
    e_ex = each(lambda a, b: jnp.exp(a - b), ecum, ld)
    e_neg = [jnp.exp(-x) for x in ecum]
    p_ref = [jnp.exp(x) for x in cref]
    e_last = each(lambda a, b: jnp.exp(a - b), clast, cref)
    mul = lambda a, b: a * b
    q_m, q_n = each(mul, kk, e_ex), each(mul, r, e_in)
    k_b, k_k = each(mul, nb, e_neg), each(mul, k, e_neg)
    lhs = each(lambda a, b: jnp.concatenate([stack(a), stack(b)], axis=0), q_m, q_n)
    rhs = each(lambda a, b: jnp.concatenate([stack(a), stack(b)], axis=0), k_b, k_k)
    pm = each(lambda a, b: jnp.where(valid, _mm(a, b, NT), 0.0), lhs, rhs)
    m_b = [x[:S, :S] for x in pm]
    m_k, n_b, n_k = ([_bf(x[rs, cs_]) for x in pm] for rs, cs_ in
                     ((slice(0, S), slice(S, 2 * S)), (slice(S, 2 * S), slice(0, S)), (slice(S, 2 * S), slice(S, 2 * S))))
    tm = [jnp.where(eye, 1.0, 0.0) + x for x in m_b]
    xp = m_b
    for _ in range(5):
        xp = [_mm(xb, xb) for xb in (_bf(x) for x in xp)]
        tm = each(lambda t, x: t + _mm(_bf(t), _bf(x)), tm, xp)
    sv = [stack(x) for x in v]
    mkv = each(lambda a, b: _bf(_mm(a, b)), m_k, sv)
    qm0 = each(lambda a, b: stack(a * b), q_m, p_ref)
    wu = each(lambda t, a, b: _bf(_mm(_bf(t), jnp.concatenate([a, b], axis=1))), tm, qm0, mkv)
    kbl_t = each(lambda a, b: _bf(_stack(a * b, lane_lo).T), k_b, e_last)
    kkl_t = each(lambda a, b: _bf(_stack(a * b, lane_lo).T), k_k, e_last)
    gc = each(_mm, kbl_t, wu)
    kv = each(_mm, kkl_t, sv)
    nw = each(_mm, n_b, wu)
    nv = each(_mm, n_k, sv)
    for c in cs:
        sl = slice(c * L, (c + 1) * L)
        qh_ref[0, sl, :] = _bf(r[c] * (e_in[c] * p_ref[c]) + _unstack(nw[c][:, :S]))
        yi_ref[0, sl, :] = _bf(_unstack(nw[c][:, S:] + nv[c]))
        g_ref[0, 0, c] = _bf(_unstack(gc[c][:, :S] + jnp.where(eye, jnp.exp(clast[c]), 0.0)))
        c_ref[0, 0, c] = _bf(_unstack(gc[c][:, S:] + kv[c]))


def _rwkv_chunks(r, ld, k, v, kk, nb, nc=16):
    B, T, D = r.shape
    L = CHUNK
    tt = nc * L
    nch = T // L
    nat = pl.BlockSpec((1, tt, LANES), lambda b, p, i: (b, i, p))
    op = pl.BlockSpec((1, 1, nc, L, LANES), lambda b, p, i: (b, p, i, 0, 0))
    nat_shape = lambda dt: jax.ShapeDtypeStruct((B, T, D), dt)
    op_shape = lambda dt: jax.ShapeDtypeStruct((B, N_PAIRS, nch, L, LANES), dt)
    return pl.pallas_call(
        functools.partial(_rwkv_chunk_body, nc=nc), grid=(B, N_PAIRS, T // tt),
        in_specs=[nat] * 6, out_specs=[nat, nat, op, op],
        out_shape=[nat_shape(BF16), nat_shape(BF16), op_shape(BF16), op_shape(BF16)],
        compiler_params=_params("parallel", "parallel", "parallel"), name="rwkv_chunks")(r, ld, k, v, kk, nb)


def _rwkv_out_body(qh_ref, yi_ref, g_ref, c_ref, x_ref, bonus_ref, gate_ref, gw_ref, gb_ref, wo_ref,
                   o_ref, h_ref, y_ref, *, nc):
    L = CHUNK
    lane_lo = lax.broadcasted_iota(jnp.int32, (L, LANES), 1) < HEAD_DIM

    @pl.when(pl.program_id(1) == 0)
    def _():
        h_ref[...] = jnp.zeros_like(h_ref)

    def chunk(c, hs):
        sl = pl.ds(pl.multiple_of(c * L, L), L)
        out = []
        for p in range(N_PAIRS):
            lanes = slice(p * LANES, (p + 1) * LANES)
            lhs = jnp.concatenate([qh_ref[0, sl, lanes], _stack(g_ref[0, p, c], lane_lo)], axis=0)
            prod = _mm(lhs, _bf(hs[p]))
            y_ref[sl, lanes] = prod[:L] + yi_ref[0, sl, lanes].astype(F32)
            out.append(prod[L:] + _stack(c_ref[0, p, c], lane_lo).astype(F32))
        return tuple(out)

    hs = lax.fori_loop(0, nc, chunk, tuple(h_ref[p] for p in range(N_PAIRS)))
    for p in range(N_PAIRS):
        h_ref[p] = hs[p]

    ones = _head_ones()
    y = y_ref[...]
    mean = _head_sum(y, ones) * (1.0 / HEAD_DIM)
    d = y - mean
    var = _head_sum(d * d, ones) * (1.0 / HEAD_DIM)
    yn = d * lax.rsqrt(var + GN_EPS) * gw_ref[...] + gb_ref[...]
    z = _bf((yn + bonus_ref[0].astype(F32)) * gate_ref[0].astype(F32))
    o_ref[0] = x_ref[0] + _mm(z, wo_ref[...])


def _rwkv_out(qh, yi, g, c, x, bonus, gate, gn_w, gn_b, w_out, nc=8):
    B, T, D = qh.shape
    L = CHUNK
    tt = nc * L
    big = pl.BlockSpec((1, tt, D), lambda b, i: (b, i, 0))
    op = pl.BlockSpec((1, N_PAIRS, nc, L, LANES), lambda b, i: (b, 0, i, 0, 0))
    ins = [qh, yi, g, c, x, bonus, gate, gn_w.reshape(1, D), gn_b.reshape(1, D), w_out.astype(BF16)]
    return pl.pallas_call(
        functools.partial(_rwkv_out_body, nc=nc), grid=(B, T // tt),
        in_specs=[big, big, op, op, big, big, big] + [_full(a.shape) for a in ins[7:]],
        out_specs=big, out_shape=jax.ShapeDtypeStruct((B, T, D), F32),
        scratch_shapes=[pltpu.VMEM((N_PAIRS, 2 * HEAD_DIM, LANES), F32), pltpu.VMEM((tt, D), F32)],
        compiler_params=_params("parallel", "arbitrary"), name="rwkv_out")(*ins)


def _head_norm_rope_t(xt, gain, cos, sin):
    heads = []
    for h in range(N_HEADS):
        xh = xt[h * HEAD_DIM:(h + 1) * HEAD_DIM, :]
        xh = xh * lax.rsqrt(jnp.mean(xh * xh, axis=0, keepdims=True) + RMS_EPS) * gain
        x1, x2 = xh[:ROPE_HALF], xh[ROPE_HALF:ROPE_DIM]
        heads += [x1 * cos - x2 * sin, x2 * cos + x1 * sin, xh[ROPE_DIM:]]
    return jnp.concatenate(heads, axis=0)


def _moba_pre_body(x_ref, nkv_ref, nb_ref, wkv_ref, wq_ref, wg_ref, kn_ref, qn_ref, cos_ref, sin_ref,
                   k_ref, km_ref, qt_ref, vt_ref, g_ref):
    x = x_ref[0]
    xn = x * lax.rsqrt(jnp.mean(x * x, axis=-1, keepdims=True) + RMS_EPS)
    h_kv = (xn * nkv_ref[...]).astype(BF16)
    h_b = (xn * nb_ref[...]).astype(BF16)
    cos, sin = cos_ref[...], sin_ref[...]
    tile = lambda g: jnp.concatenate([g] * (x.shape[0] // LANES), axis=1)
    kt = _head_norm_rope_t(_mm(wkv_ref[:D_MODEL, :], h_kv, NT), tile(kn_ref[...]), cos, sin)
    tag = lax.broadcasted_iota(jnp.int32, (TAG_LANES, x.shape[0]), 0) == pl.program_id(1)
    tag = jnp.where(tag, 1.0, 0.0)
    for h in range(N_HEADS):
        k_aug = jnp.concatenate([kt[h * HEAD_DIM:(h + 1) * HEAD_DIM], tag], axis=0).T
        k_ref[0, h] = k_aug.astype(BF16)
        km_ref[0, 0, h:h + 1, :] = jnp.mean(k_aug, axis=0, keepdims=True)
    qt = _head_norm_rope_t(_mm(wq_ref[...], h_b, NT), tile(qn_ref[...]), cos, sin)
    qt_ref[0, 0] = (qt * (LOG2E / HEAD_DIM ** 0.5)).astype(BF16)
    gate = _mm(wg_ref[...], h_b, NT)
    g_ref[0, 0] = _bf(gate * _sigmoid(gate))
    vt_ref[0, 0] = _bf(_mm(wkv_ref[D_MODEL:, :], h_kv, NT))


def _rope_tables_t(T):
    inv_freq = jnp.power(jnp.float32(ROPE_THETA), -jnp.arange(0, ROPE_DIM, 2, dtype=F32) / ROPE_DIM)
    ang = inv_freq[:, None] * jnp.arange(T, dtype=F32)[None, :]
    return jnp.cos(ang), jnp.sin(ang)


def _moba_pre(x, norm_kv, norm_b, w_kv, w_in, k_norm, q_norm):
    B, T, D = x.shape
    tm = MOBA_BLOCK
    nb = T // tm
    cos, sin = _rope_tables_t(T)
    bcast = lambda g: jnp.broadcast_to(g.reshape(HEAD_DIM, 1), (HEAD_DIM, LANES))
    ins = [x, norm_kv.reshape(1, D), norm_b.reshape(1, D), w_kv.T.astype(BF16), w_in[:, :D].T.astype(BF16),
           w_in[:, D:].T.astype(BF16), bcast(k_norm), bcast(q_norm), cos, sin]
    big = pl.BlockSpec((1, tm, D), lambda b, i: (b, i, 0))
    tab = pl.BlockSpec((ROPE_HALF, tm), lambda b, i: (0, i))
    tr = pl.BlockSpec((1, 1, D, tm), lambda b, i: (b, i, 0, 0))
    in_specs = [big] + [_full(a.shape) for a in ins[1:8]] + [tab, tab]
    assert tm == MOBA_BLOCK and nb <= TAG_LANES
    out_specs = [pl.BlockSpec((1, N_HEADS, tm, LANES), lambda b, i: (b, 0, i, 0)),
                 pl.BlockSpec((1, 1, N_HEADS, LANES), lambda b, i: (b, i, 0, 0)), tr, tr, tr]
    out_shape = [jax.ShapeDtypeStruct((B, N_HEADS, T, LANES), BF16), jax.ShapeDtypeStruct((B, nb, N_HEADS, LANES), F32),
                 jax.ShapeDtypeStruct((B, nb, D, tm), BF16), jax.ShapeDtypeStruct((B, nb, D, tm), BF16),
                 jax.ShapeDtypeStruct((B, nb, D, tm), BF16)]
    return pl.pallas_call(
        _moba_pre_body, grid=(B, nb), in_specs=in_specs, out_specs=out_specs, out_shape=out_shape,
        compiler_params=_params("parallel", "parallel"), name="moba_pre")(*ins)


def _attn_schedule(nb):
    items = []
    for d in range(1, nb):
        diag = [(j, j + d) for j in range(nb - d)]
        diag += [(0, nb)] * (-len(diag) % ATTN_GROUP)
        items += diag
    n_trips = -(-len(items) // (ATTN_GROUP * ATTN_STAGES))
    items += [(0, nb)] * (ATTN_GROUP * (n_trips * ATTN_STAGES + 1) - len(items))
    return [v for it in items for v in it], n_trips


def _moba_attn_body(tab_ref, k_ref, km_ref, qt_ref, vt_ref, o_ref, qa_ref, acc_ref, m_ref, sa_ref, sb_ref,
                    *, nb, n_trips):
    BS = MOBA_BLOCK
    G = ATTN_GROUP
    heads = range(2)
    ones = jnp.ones((SUM_ROWS, BS), BF16)
    zeros_tag = jnp.zeros((TAG_LANES, BS), BF16)
    rows = lambda h: slice(h * HEAD_DIM, (h + 1) * HEAD_DIM)
    v_aug = lambda j, h: jnp.concatenate([vt_ref[0, j, rows(h), :], ones], axis=0)
    k_blk = lambda j, h: k_ref[0, h, pl.ds(pl.multiple_of(j * BS, BS), BS), :]

    blk = lax.broadcasted_iota(jnp.int32, (nb, BS), 0)
    blk_f = blk.astype(F32)
    km = [_split(km_ref[0, h]) for h in heads]
    pad = [jnp.full((TAG_LANES - nb, BS), NEG, F32)] if nb < TAG_LANES else []

    def gate(g, carry):
        ih = [(g * GATE_GROUP + e, h) for e in range(GATE_GROUP) for h in heads]
        q = [qt_ref[0, i, rows(h), :] for i, h in ih]
        qz = [jnp.concatenate([x, zeros_tag], axis=0) for x in q]
        gs = [_mm(km[h][0], x) + _mm(km[h][1], x) for x, (i, h) in zip(qz, ih)]
        work = [jnp.where(blk < i, x, -jnp.inf) for x, (i, h) in zip(gs, ih)]
        f = [jnp.full((nb, BS), NEG, F32) for _ in ih]
        for _ in range(MOBA_TOPK):
            mx = [jnp.max(x, axis=0, keepdims=True) for x in work]
            first = [jnp.min(jnp.where(w == m, blk_f, float(nb)), axis=0, keepdims=True) for w, m in zip(work, mx)]
            hit = [blk_f == x for x in first]
            f = [jnp.where(h_, jnp.where(m > -jnp.inf, 0.0, f_), f_) for h_, m, f_ in zip(hit, mx, f)]
            work = [jnp.where(h_, -jnp.inf, w) for h_, w in zip(hit, work)]
        for x, f_, (i, h) in zip(q, f, ih):
            qa_ref[i, h] = jnp.concatenate([x, _bf(jnp.concatenate([f_] + pad, axis=0))], axis=0)
        return carry

    lax.fori_loop(0, nb // GATE_GROUP, gate, 0)
    for h in heads:
        qa_ref[nb, h] = jnp.concatenate([jnp.zeros((HEAD_DIM, BS), BF16), jnp.full((TAG_LANES, BS), NEG, BF16)], axis=0)
    acc_ref[...] = jnp.zeros_like(acc_ref)
    m_ref[...] = jnp.full(m_ref.shape, 0.5 * NEG, F32)

    key_pos = lax.broadcasted_iota(jnp.int32, (BS, BS), 0)
    qry_pos = lax.broadcasted_iota(jnp.int32, (BS, BS), 1)
    eh = [(e, h) for e in range(G) for h in heads]

    def scores_one(item, causal, g, dst_ref, c, e, h):
        j, i = item(g, e)
        q_op = jnp.concatenate([qt_ref[0, i, rows(h), :], zeros_tag], axis=0) if causal else qa_ref[i, h]
        dst_ref[c] = _mm(k_blk(j, h), q_op)

    def consume_one(item, causal, src_ref, g, c, e, h):
        j, i = item(g, e)
        s = jnp.where(key_pos <= qry_pos, src_ref[c], NEG) if causal else src_ref[c]
        m_old = m_ref[i, h]
        m_new = jnp.maximum(m_old, jnp.max(s, axis=0, keepdims=True))
        acc = jnp.exp2(m_old - m_new) * acc_ref[i, h] + _mm(v_aug(j, h), _bf(jnp.exp2(s - m_new)))
        m_ref[i, h] = m_new
        acc_ref[i, h] = acc

    def stage(item, causal, g, src_ref, dst_ref):
        for c, (e, h) in enumerate(eh):
            scores_one(item, causal, g + 1, dst_ref, c, e, h)
            consume_one(item, causal, src_ref, g, c, e, h)

    def stream(item, causal, n_trips, stages):
        for c, (e, h) in enumerate(eh):
            scores_one(item, causal, 0, sa_ref, c, e, h)

        def trip(t, carry):
            g0 = t * stages
            for u in range(0, stages, 2):
                stage(item, causal, g0 + u, sa_ref, sb_ref)
                stage(item, causal, g0 + u + 1, sb_ref, sa_ref)
            return carry

        lax.fori_loop(0, n_trips, trip, 0)

    own_item = lambda g, e: (jnp.minimum(g * G + e, nb - 1),) * 2
    stream(own_item, True, nb // (2 * G), 2)
    stream(lambda g, e: (tab_ref[2 * (g * G + e)], tab_ref[2 * (g * G + e) + 1]), False, n_trips, ATTN_STAGES)

    def finish(g, carry):
        for e in range(G):
            i = g * G + e
            acc = [acc_ref[i, h] for h in heads]
            o_ref[0, i] = _bf(jnp.concatenate([a[:HEAD_DIM] / a[HEAD_DIM:HEAD_DIM + 1] for a in acc], axis=0))
        return carry

    lax.fori_loop(0, nb // G, finish, 0)


def _moba_attn(k, kmean, qt, vt):
    B, H, T, _ = k.shape
    BS = MOBA_BLOCK
    nb = T // BS
    assert nb % GATE_GROUP == 0 and nb % (2 * ATTN_GROUP) == 0 and ATTN_STAGES % 2 == 0
    table, n_trips = _attn_schedule(nb)
    in_specs = [pl.BlockSpec(memory_space=pltpu.SMEM),
                pl.BlockSpec((1, 2, T, LANES), lambda b, p: (b, p, 0, 0)),
                pl.BlockSpec((1, 2, nb, LANES), lambda b, p: (b, p, 0, 0)),
                pl.BlockSpec((1, nb, LANES, BS), lambda b, p: (b, 0, p, 0)),
                pl.BlockSpec((1, nb, LANES, BS), lambda b, p: (b, 0, p, 0))]
    chains = 2 * ATTN_GROUP
    scratch = [pltpu.VMEM((nb + 1, 2, LANES, BS), BF16),
               pltpu.VMEM((nb + 1, 2, HEAD_DIM + SUM_ROWS, BS), F32),
               pltpu.VMEM((nb + 1, 2, 1, BS), F32),
               pltpu.VMEM((chains, BS, BS), F32), pltpu.VMEM((chains, BS, BS), F32)]
    return pl.pallas_call(
        functools.partial(_moba_attn_body, nb=nb, n_trips=n_trips), grid=(B, N_PAIRS), in_specs=in_specs,
        out_specs=pl.BlockSpec((1, nb, LANES, BS), lambda b, p: (b, 0, p, 0)),
        out_shape=jax.ShapeDtypeStruct((B, nb, H * HEAD_DIM, BS), BF16), scratch_shapes=scratch,
        compiler_params=_params("parallel", "parallel"), name="moba_attn")(
            jnp.asarray(table, jnp.int32), k, kmean, qt, vt)


def _moba_post_body(x_ref, ot_ref, gt_ref, wo_ref, out_ref):
    BS = MOBA_BLOCK
    for u in range(ot_ref.shape[1]):
        zt = _bf(ot_ref[0, u].astype(F32) * gt_ref[0, u].astype(F32))
        out_ref[0, u * BS:(u + 1) * BS, :] = x_ref[0, u * BS:(u + 1) * BS, :] + _mm(zt, wo_ref[...], TN)


def _moba_post(x, ot, gt, w_out, tm=512):
    B, T, D = x.shape
    big = pl.BlockSpec((1, tm, D), lambda b, i: (b, i, 0))
    tr = pl.BlockSpec((1, tm // MOBA_BLOCK, D, MOBA_BLOCK), lambda b, i: (b, i, 0, 0))
    return pl.pallas_call(
        _moba_post_body, grid=(B, T // tm), in_specs=[big, tr, tr, _full((D, D))], out_specs=big,
        out_shape=jax.ShapeDtypeStruct((B, T, D), F32),
        compiler_params=_params("parallel", "parallel"), name="moba_post")(x, ot, gt, w_out.astype(BF16))


def _rwkv_layer(x, norm_a, mu, w_in, w0, w2, a0, a2, kkw, kaw, rk, gn_w, gn_b, w_out):
    r, ld, k, v, kk, nb, bonus, g = _rwkv_pre(x, norm_a, mu, w_in, w0, w2, a0, a2, kkw, kaw, rk.reshape(-1))
    qh, yi, gm, cm = _rwkv_chunks(r, ld, k, v, kk, nb)
    return _rwkv_out(qh, yi, gm, cm, x, bonus, g, gn_w, gn_b, w_out)


def _moba_layer(x, norm_kv, w_kv, k_norm, norm_b, w_in, q_norm, w_out):
    B, T, D = x.shape
    k, kmean, qt, vt, g = _moba_pre(x, norm_kv, norm_b, w_kv, w_in, k_norm, q_norm)
    o = _moba_attn(k, kmean.transpose(0, 2, 1, 3), qt, vt)
    return _moba_post(x, o, g, w_out)


def kernel(x, norm_a, mu_a, w_in_a, w0_a, w2_a, a0_a, a2_a, kk_a, ka_a, rk_a, gn_w_a, gn_b_a, w_out_a, norm_kv, w_kv, k_norm, norm_b, w_in_b, q_norm_b, w_out_b):
    x = _rwkv_layer(x, norm_a[0], mu_a[0], w_in_a[0], w0_a[0], w2_a[0], a0_a[0], a2_a[0], kk_a[0], ka_a[0],
                    rk_a[0], gn_w_a[0], gn_b_a[0], w_out_a[0])
    return _moba_layer(x, norm_kv, w_kv, k_norm, norm_b[0], w_in_b[0], q_norm_b[0], w_out_b[0])
```

```python
import functools

import jax
import jax.numpy as jnp
from jax import lax
from jax.experimental import pallas as pl
from jax.experimental.pallas import tpu as pltpu

F32 = jnp.float32
BF16 = jnp.bfloat16

D_MODEL = 1024
HEAD_DIM = 64
N_HEADS = D_MODEL // HEAD_DIM
LANES = 128
N_PAIRS = D_MODEL // LANES
LORA = 64
MOBA_BLOCK = 256
MOBA_TOPK = 3
ROPE_THETA = 500000.0
ROPE_DIM = HEAD_DIM // 4
ROPE_HALF = ROPE_DIM // 2
RMS_EPS = 1e-6
GN_EPS = 64e-5
NEG = -1e30
LOG2E = 1.4426950408889634
DECAY_SCALE = 0.6065306597126334
SUM_ROWS = 16
TAG_LANES = LANES - HEAD_DIM
ATTN_GROUP = 4
ATTN_STAGES = 4
GATE_GROUP = 4
CHUNK = 64
VMEM_LIMIT = 56 * 1024 * 1024

NN = (((1,), (0,)), ((), ()))
NT = (((1,), (1,)), ((), ()))
TN = (((0,), (0,)), ((), ()))


def _mm(a, b, dims=NN):
    return lax.dot_general(a, b, dims, preferred_element_type=F32)


def _split(x):
    hi = x.astype(BF16)
    lo = (x - hi.astype(F32)).astype(BF16)
    return hi, lo


def _bf(x):
    return x.astype(BF16)


def _head_ones():
    r = lax.broadcasted_iota(jnp.int32, (LANES, LANES), 0) // HEAD_DIM
    c = lax.broadcasted_iota(jnp.int32, (LANES, LANES), 1) // HEAD_DIM
    return jnp.where(r == c, 1.0, 0.0).astype(BF16)


def _head_sum(x, ones):
    parts = [_mm(_bf(x[:, p * LANES:(p + 1) * LANES]), ones) for p in range(x.shape[1] // LANES)]
    return jnp.concatenate(parts, axis=1)


def _sigmoid(x):
    return 1.0 / (1.0 + jnp.exp(-x))


def _rms(x, g):
    return x * lax.rsqrt(jnp.mean(x * x, axis=-1, keepdims=True) + RMS_EPS) * g


def _params(*sem):
    return pltpu.CompilerParams(dimension_semantics=sem, vmem_limit_bytes=VMEM_LIMIT)


def _full(shape):
    return pl.BlockSpec(shape, lambda *_: (0,) * len(shape))


def _rwkv_pre_body(x_ref, xp_ref, na_ref, mu_ref, wr_ref, wwd_ref, wk_ref, wv_ref, wad_ref, wg_ref,
                   w0_ref, w2_ref, a0_ref, a2_ref, kkw_ref, kaw_ref, rk_ref,
                   r_ref, ld_ref, k_ref, v_ref, kk_ref, nb_ref, bonus_ref, g_ref):
    x = x_ref[0]
    gn = na_ref[...]
    h = _rms(x, gn)
    hp = _rms(xp_ref[0], gn)
    prev = jnp.where(pl.program_id(1) == 0, 0.0, hp[7:8, :])
    rolled = pltpu.roll(h, 1, axis=0)
    first = lax.broadcasted_iota(jnp.int32, h.shape, 0) == 0
    xx = jnp.where(first, prev, rolled) - h

    def mix(i):
        return (h + xx * mu_ref[i:i + 1, :]).astype(BF16)

    r = _mm(mix(0), wr_ref[...])
    wdn = _mm(mix(1), wwd_ref[...])
    k = _mm(mix(2), wk_ref[...])
    v = _mm(mix(3), wv_ref[...])
    adn = _mm(mix(4), wad_ref[...])
    gp = _mm(mix(5), wg_ref[...])

    ld_ref[0] = -DECAY_SCALE * _sigmoid(w0_ref[...] + _mm(jnp.tanh(wdn).astype(BF16), w2_ref[...]))
    a = _sigmoid(a0_ref[...] + _mm(adn.astype(BF16), a2_ref[...]))

    ones = _head_ones()
    kk = k * kkw_ref[...]
    kk = kk * lax.rsqrt(jnp.maximum(_head_sum(kk * kk, ones), 1e-24))
    k2 = k * (1.0 + (a - 1.0) * kaw_ref[...])
    r_ref[0] = r
    k_ref[0] = k2
    v_ref[0] = v
    kk_ref[0] = kk
    nb_ref[0] = -(kk * a)
    bonus_ref[0] = _bf(_head_sum(r * k2 * rk_ref[...], ones) * v)
    g_ref[0] = _bf(gp * _sigmoid(gp))


def _rwkv_pre(x, norm_a, mu, w_in, w0, w2, a0, a2, kkw, kaw, rk, tm=256):
    B, T, D = x.shape
    offs = [0, D, D + LORA, 2 * D + LORA, 3 * D + LORA, 3 * D + 2 * LORA, 4 * D + 2 * LORA]
    ws = [w_in[:, offs[i]:offs[i + 1]].astype(BF16) for i in range(6)]
    row = lambda a: a.reshape(1, D)
    big = pl.BlockSpec((1, tm, D), lambda b, i: (b, i, 0))
    prev = pl.BlockSpec((1, 8, D), lambda b, i: (b, jnp.maximum(i * (tm // 8) - 1, 0), 0))
    ins = [x, x, row(norm_a), mu] + ws + [row(w0), w2.astype(BF16), row(a0), a2.astype(BF16),
                                          row(kkw), row(kaw), row(rk)]
    in_specs = [big, prev] + [_full(a.shape) for a in ins[2:]]
    out = lambda dt: jax.ShapeDtypeStruct((B, T, D), dt)
    return pl.pallas_call(
        _rwkv_pre_body, grid=(B, T // tm), in_specs=in_specs, out_specs=[big] * 8,
        out_shape=[out(F32)] * 6 + [out(BF16)] * 2,
        compiler_params=_params("parallel", "parallel"), name="rwkv_pre")(*ins)


def _stack(x, lane_lo):
    return jnp.concatenate([jnp.where(lane_lo, x, 0.0), jnp.where(lane_lo, 0.0, x)], axis=0)


def _unstack(z):
    half = z.shape[0] // 2
    return z[:half] + z[half:]


def _rwkv_chunk_body(r_ref, ld_ref, k_ref, v_ref, kk_ref, nb_ref, qh_ref, yi_ref, g_ref, c_ref, *, nc):
    L = CHUNK
    S = 2 * L
    lane_lo = lax.broadcasted_iota(jnp.int32, (L, LANES), 1) < HEAD_DIM
    row = lax.broadcasted_iota(jnp.int32, (L, LANES), 0)
    pi = lax.broadcasted_iota(jnp.int32, (2 * S, 2 * S), 0)
    pj = lax.broadcasted_iota(jnp.int32, (2 * S, 2 * S), 1)
    same_head = ((pi >> 6) & 1) == ((pj >> 6) & 1)
    t_i, s_j = pi & (L - 1), pj & (L - 1)
    valid = same_head & ((s_j < t_i) | ((pi >= S) & (s_j == t_i)))
    ei = lax.broadcasted_iota(jnp.int32, (S, S), 0)
    ej = lax.broadcasted_iota(jnp.int32, (S, S), 1)
    eye = ei == ej
    stack = lambda x: _bf(_stack(x, lane_lo))

    cs = range(nc)
    each = lambda f, *xs: [f(*a) for a in zip(*xs)]
    load = lambda ref: [ref[0, c * L:(c + 1) * L, :] for c in cs]
    r, ld, k, v, kk, nb = (load(ref) for ref in (r_ref, ld_ref, k_ref, v_ref, kk_ref, nb_ref))

    def cumsum(x):
        d = 1
        while d < L:
            x = x + jnp.where(row >= d, pltpu.roll(x, d, axis=0), 0.0)
            d *= 2
        return x

    cum = each(cumsum, ld)
    cref = [x[L // 2 - 1:L // 2, :] for x in cum]
    clast = [x[L - 1:L, :] for x in cum]
    ecum = each(lambda a, b: a - b, cum, cref)
    e_in = [jnp.exp(x) for x in ecum]
    e_ex = each(lambda a, b: jnp.exp(a - b), ecum, ld)
    e_neg = [jnp.exp(-x) for x in ecum]
    p_ref = [jnp.exp(x) for x in cref]
    e_last = each(lambda a, b: jnp.exp(a - b), clast, cref)
    mul = lambda a, b: a * b
    q_m, q_n = each(mul, kk, e_ex), each(mul, r, e_in)
    k_b, k_k = each(mul, nb, e_neg), each(mul, k, e_neg)
    lhs = each(lambda a, b: jnp.concatenate([stack(a), stack(b)], axis=0), q_m, q_n)
    rhs = each(lambda a, b: jnp.concatenate([stack(a), stack(b)], axis=0), k_b, k_k)
    pm = each(lambda a, b: jnp.where(valid, _mm(a, b, NT), 0.0), lhs, rhs)
    m_b = [x[:S, :S] for x in pm]
    m_k, n_b, n_k = ([_bf(x[rs, cs_]) for x in pm] for rs, cs_ in
                     ((slice(0, S), slice(S, 2 * S)), (slice(S, 2 * S), slice(0, S)), (slice(S, 2 * S), slice(S, 2 * S))))
    tm = [jnp.where(eye, 1.0, 0.0) + x for x in m_b]
    xp = m_b
    for _ in range(5):
        xp = [_mm(xb, xb) for xb in (_bf(x) for x in xp)]
        tm = each(lambda t, x: t + _mm(_bf(t), _bf(x)), tm, xp)
    sv = [stack(x) for x in v]
    mkv = each(lambda a, b: _bf(_mm(a, b)), m_k, sv)
    qm0 = each(lambda a, b: stack(a * b), q_m, p_ref)
    wu = each(lambda t, a, b: _bf(_mm(_bf(t), jnp.concatenate([a, b], axis=1))), tm, qm0, mkv)
    kbl_t = each(lambda a, b: _bf(_stack(a * b, lane_lo).T), k_b, e_last)
    kkl_t = each(lambda a, b: _bf(_stack(a * b, lane_lo).T), k_k, e_last)
    gc = each(_mm, kbl_t, wu)
    kv = each(_mm, kkl_t, sv)
    nw = each(_mm, n_b, wu)
    nv = each(_mm, n_k, sv)
    for c in cs:
        sl = slice(c * L, (c + 1) * L)
        qh_ref[0, sl, :] = _bf(r[c] * (e_in[c] * p_ref[c]) + _unstack(nw[c][:, :S]))
        yi_ref[0, sl, :] = _bf(_unstack(nw[c][:, S:] + nv[c]))
        g_ref[0, 0, c] = _bf(_unstack(gc[c][:, :S] + jnp.where(eye, jnp.exp(clast[c]), 0.0)))
        c_ref[0, 0, c] = _bf(_unstack(gc[c][:, S:] + kv[c]))


def _rwkv_chunks(r, ld, k, v, kk, nb, nc=16):
    B, T, D = r.shape
    L = CHUNK
    tt = nc * L
    nch = T // L
    nat = pl.BlockSpec((1, tt, LANES), lambda b, p, i: (b, i, p))
    op = pl.BlockSpec((1, 1, nc, L, LANES), lambda b, p, i: (b, p, i, 0, 0))
    nat_shape = lambda dt: jax.ShapeDtypeStruct((B, T, D), dt)
    op_shape = lambda dt: jax.ShapeDtypeStruct((B, N_PAIRS, nch, L, LANES), dt)
    return pl.pallas_call(
        functools.partial(_rwkv_chunk_body, nc=nc), grid=(B, N_PAIRS, T // tt),
        in_specs=[nat] * 6, out_specs=[nat, nat, op, op],
        out_shape=[nat_shape(BF16), nat_shape(BF16), op_shape(BF16), op_shape(BF16)],
        compiler_params=_params("parallel", "parallel", "parallel"), name="rwkv_chunks")(r, ld, k, v, kk, nb)


def _rwkv_out_body(qh_ref, yi_ref, g_ref, c_ref, x_ref, bonus_ref, gate_ref, gw_ref, gb_ref, wo_ref,
                   o_ref, h_ref, y_ref, *, nc):
    L = CHUNK
    lane_lo = lax.broadcasted_iota(jnp.int32, (L, LANES), 1) < HEAD_DIM

    @pl.when(pl.program_id(1) == 0)
    def _():
        h_ref[...] = jnp.zeros_like(h_ref)

    def chunk(c, hs):
        sl = pl.ds(pl.multiple_of(c * L, L), L)
        out = []
        for p in range(N_PAIRS):
            lanes = slice(p * LANES, (p + 1) * LANES)
            lhs = jnp.concatenate([qh_ref[0, sl, lanes], _stack(g_ref[0, p, c], lane_lo)], axis=0)
            prod = _mm(lhs, _bf(hs[p]))
            y_ref[sl, lanes] = prod[:L] + yi_ref[0, sl, lanes].astype(F32)
            out.append(prod[L:] + _stack(c_ref[0, p, c], lane_lo).astype(F32))
        return tuple(out)

    hs = lax.fori_loop(0, nc, chunk, tuple(h_ref[p] for p in range(N_PAIRS)))
    for p in range(N_PAIRS):
        h_ref[p] = hs[p]

    ones = _head_ones()
    y = y_ref[...]
    mean = _head_sum(y, ones) * (1.0 / HEAD_DIM)
    d = y - mean
    var = _head_sum(d * d, ones) * (1.0 / HEAD_DIM)
    yn = d * lax.rsqrt(var + GN_EPS) * gw_ref[...] + gb_ref[...]
    z = _bf((yn + bonus_ref[0].astype(F32)) * gate_ref[0].astype(F32))
    o_ref[0] = x_ref[0] + _mm(z, wo_ref[...])


def _rwkv_out(qh, yi, g, c, x, bonus, gate, gn_w, gn_b, w_out, nc=8):
    B, T, D = qh.shape
    L = CHUNK
    tt = nc * L
    big = pl.BlockSpec((1, tt, D), lambda b, i: (b, i, 0))
    op = pl.BlockSpec((1, N_PAIRS, nc, L, LANES), lambda b, i: (b, 0, i, 0, 0))
    ins = [qh, yi, g, c, x, bonus, gate, gn_w.reshape(1, D), gn_b.reshape(1, D), w_out.astype(BF16)]
    return pl.pallas_call(
        functools.partial(_rwkv_out_body, nc=nc), grid=(B, T // tt),
        in_specs=[big, big, op, op, big, big, big] + [_full(a.shape) for a in ins[7:]],
        out_specs=big, out_shape=jax.ShapeDtypeStruct((B, T, D), F32),
        scratch_shapes=[pltpu.VMEM((N_PAIRS, 2 * HEAD_DIM, LANES), F32), pltpu.VMEM((tt, D), F32)],
        compiler_params=_params("parallel", "arbitrary"), name="rwkv_out")(*ins)


def _head_norm_rope_t(xt, gain, cos, sin):
    heads = []
    for h in range(N_HEADS):
        xh = xt[h * HEAD_DIM:(h + 1) * HEAD_DIM, :]
        xh = xh * lax.rsqrt(jnp.mean(xh * xh, axis=0, keepdims=True) + RMS_EPS) * gain
        x1, x2 = xh[:ROPE_HALF], xh[ROPE_HALF:ROPE_DIM]
        heads += [x1 * cos - x2 * sin, x2 * cos + x1 * sin, xh[ROPE_DIM:]]
    return jnp.concatenate(heads, axis=0)


def _moba_pre_body(x_ref, nkv_ref, nb_ref, wkv_ref, wq_ref, wg_ref, kn_ref, qn_ref, cos_ref, sin_ref,
                   k_ref, km_ref, qt_ref, vt_ref, g_ref):
    x = x_ref[0]
    xn = x * lax.rsqrt(jnp.mean(x * x, axis=-1, keepdims=True) + RMS_EPS)
    h_kv = (xn * nkv_ref[...]).astype(BF16)
    h_b = (xn * nb_ref[...]).astype(BF16)
    cos, sin = cos_ref[...], sin_ref[...]
    tile = lambda g: jnp.concatenate([g] * (x.shape[0] // LANES), axis=1)
    kt = _head_norm_rope_t(_mm(wkv_ref[:D_MODEL, :], h_kv, NT), tile(kn_ref[...]), cos, sin)
    tag = lax.broadcasted_iota(jnp.int32, (TAG_LANES, x.shape[0]), 0) == pl.program_id(1)
    tag = jnp.where(tag, 1.0, 0.0)
    for h in range(N_HEADS):
        k_aug = jnp.concatenate([kt[h * HEAD_DIM:(h + 1) * HEAD_DIM], tag], axis=0).T
        k_ref[0, h] = k_aug.astype(BF16)
        km_ref[0, 0, h:h + 1, :] = jnp.mean(k_aug, axis=0, keepdims=True)
    qt = _head_norm_rope_t(_mm(wq_ref[...], h_b, NT), tile(qn_ref[...]), cos, sin)
    qt_ref[0, 0] = (qt * (LOG2E / HEAD_DIM ** 0.5)).astype(BF16)
    gate = _mm(wg_ref[...], h_b, NT)
    g_ref[0, 0] = _bf(gate * _sigmoid(gate))
    vt_ref[0, 0] = _bf(_mm(wkv_ref[D_MODEL:, :], h_kv, NT))


def _rope_tables_t(T):
    inv_freq = jnp.power(jnp.float32(ROPE_THETA), -jnp.arange(0, ROPE_DIM, 2, dtype=F32) / ROPE_DIM)
    ang = inv_freq[:, None] * jnp.arange(T, dtype=F32)[None, :]
    return jnp.cos(ang), jnp.sin(ang)


def _moba_pre(x, norm_kv, norm_b, w_kv, w_in, k_norm, q_norm):
    B, T, D = x.shape
    tm = MOBA_BLOCK
    nb = T // tm
    cos, sin = _rope_tables_t(T)
    bcast = lambda g: jnp.broadcast_to(g.reshape(HEAD_DIM, 1), (HEAD_DIM, LANES))
    ins = [x, norm_kv.reshape(1, D), norm_b.reshape(1, D), w_kv.T.astype(BF16), w_in[:, :D].T.astype(BF16),
           w_in[:, D:].T.astype(BF16), bcast(k_norm), bcast(q_norm), cos, sin]
    big = pl.BlockSpec((1, tm, D), lambda b, i: (b, i, 0))
    tab = pl.BlockSpec((ROPE_HALF, tm), lambda b, i: (0, i))
    tr = pl.BlockSpec((1, 1, D, tm), lambda b, i: (b, i, 0, 0))
    in_specs = [big] + [_full(a.shape) for a in ins[1:8]] + [tab, tab]
    assert tm == MOBA_BLOCK and nb <= TAG_LANES
    out_specs = [pl.BlockSpec((1, N_HEADS, tm, LANES), lambda b, i: (b, 0, i, 0)),
                 pl.BlockSpec((1, 1, N_HEADS, LANES), lambda b, i: (b, i, 0, 0)), tr, tr, tr]
    out_shape = [jax.ShapeDtypeStruct((B, N_HEADS, T, LANES), BF16), jax.ShapeDtypeStruct((B, nb, N_HEADS, LANES), F32),
                 jax.ShapeDtypeStruct((B, nb, D, tm), BF16), jax.ShapeDtypeStruct((B, nb, D, tm), BF16),
                 jax.ShapeDtypeStruct((B, nb, D, tm), BF16)]
    return pl.pallas_call(
        _moba_pre_body, grid=(B, nb), in_specs=in_specs, out_specs=out_specs, out_shape=out_shape,
        compiler_params=_params("parallel", "parallel"), name="moba_pre")(*ins)


def _attn_schedule(nb):
    items = []
    for d in range(1, nb):
        diag = [(j, j + d) for j in range(nb - d)]
        diag += [(0, nb)] * (-len(diag) % ATTN_GROUP)
        items += diag
    n_trips = -(-len(items) // (ATTN_GROUP * ATTN_STAGES))
    items += [(0, nb)] * (ATTN_GROUP * (n_trips * ATTN_STAGES + 1) - len(items))
    return [v for it in items for v in it], n_trips


def _moba_attn_body(tab_ref, k_ref, km_ref, qt_ref, vt_ref, o_ref, qa_ref, acc_ref, m_ref, sa_ref, sb_ref,
                    *, nb, n_trips):
    BS = MOBA_BLOCK
    G = ATTN_GROUP
    heads = range(2)
    ones = jnp.ones((SUM_ROWS, BS), BF16)
    zeros_tag = jnp.zeros((TAG_LANES, BS), BF16)
    rows = lambda h: slice(h * HEAD_DIM, (h + 1) * HEAD_DIM)
    v_aug = lambda j, h: jnp.concatenate([vt_ref[0, j, rows(h), :], ones], axis=0)
    k_blk = lambda j, h: k_ref[0, h, pl.ds(pl.multiple_of(j * BS, BS), BS), :]

    blk = lax.broadcasted_iota(jnp.int32, (nb, BS), 0)
    blk_f = blk.astype(F32)
    km = [_split(km_ref[0, h]) for h in heads]
    pad = [jnp.full((TAG_LANES - nb, BS), NEG, F32)] if nb < TAG_LANES else []

    def gate(g, carry):
        ih = [(g * GATE_GROUP + e, h) for e in range(GATE_GROUP) for h in heads]
        q = [qt_ref[0, i, rows(h), :] for i, h in ih]
        qz = [jnp.concatenate([x, zeros_tag], axis=0) for x in q]
        gs = [_mm(km[h][0], x) + _mm(km[h][1], x) for x, (i, h) in zip(qz, ih)]
        work = [jnp.where(blk < i, x, -jnp.inf) for x, (i, h) in zip(gs, ih)]
        f = [jnp.full((nb, BS), NEG, F32) for _ in ih]
        for _ in range(MOBA_TOPK):
            mx = [jnp.max(x, axis=0, keepdims=True) for x in work]
            first = [jnp.min(jnp.where(w == m, blk_f, float(nb)), axis=0, keepdims=True) for w, m in zip(work, mx)]
            hit = [blk_f == x for x in first]
            f = [jnp.where(h_, jnp.where(m > -jnp.inf, 0.0, f_), f_) for h_, m, f_ in zip(hit, mx, f)]
            work = [jnp.where(h_, -jnp.inf, w) for h_, w in zip(hit, work)]
        for x, f_, (i, h) in zip(q, f, ih):
            qa_ref[i, h] = jnp.concatenate([x, _bf(jnp.concatenate([f_] + pad, axis=0))], axis=0)
        return carry

    lax.fori_loop(0, nb // GATE_GROUP, gate, 0)
    for h in heads:
        qa_ref[nb, h] = jnp.concatenate([jnp.zeros((HEAD_DIM, BS), BF16), jnp.full((TAG_LANES, BS), NEG, BF16)], axis=0)
        acc_ref[nb, h] = jnp.zeros(acc_ref.shape[2:], F32)
        m_ref[nb, h] = jnp.full(m_ref.shape[2:], 0.5 * NEG, F32)

    key_pos = lax.broadcasted_iota(jnp.int32, (BS, BS), 0)
    qry_pos = lax.broadcasted_iota(jnp.int32, (BS, BS), 1)
    eh = [(e, h) for e in range(G) for h in heads]

    def scores_one(item, causal, g, dst_ref, c, e, h):
        j, i = item(g, e)
        q_op = jnp.concatenate([qt_ref[0, i, rows(h), :], zeros_tag], axis=0) if causal else qa_ref[i, h]
        dst_ref[c] = _mm(k_blk(j, h), q_op)

    def consume_one(item, causal, src_ref, g, c, e, h):
        j, i = item(g, e)
        if causal:
            s = jnp.where(key_pos <= qry_pos, src_ref[c], NEG)
            m_new = jnp.max(s, axis=0, keepdims=True)
            acc = _mm(v_aug(j, h), _bf(jnp.exp2(s - m_new)))
        else:
            s = src_ref[c]
            m_old = m_ref[i, h]
            m_new = jnp.maximum(m_old, jnp.max(s, axis=0, keepdims=True))
            acc = jnp.exp2(m_old - m_new) * acc_ref[i, h] + _mm(v_aug(j, h), _bf(jnp.exp2(s - m_new)))
        m_ref[i, h] = m_new
        acc_ref[i, h] = acc

    def stage(item, causal, g, src_ref, dst_ref):
        for c, (e, h) in enumerate(eh):
            scores_one(item, causal, g + 1, dst_ref, c, e, h)
            consume_one(item, causal, src_ref, g, c, e, h)

    def stream(item, causal, n_trips, stages):
        for c, (e, h) in enumerate(eh):
            scores_one(item, causal, 0, sa_ref, c, e, h)

        def trip(t, carry):
            g0 = t * stages
            for u in range(0, stages, 2):
                stage(item, causal, g0 + u, sa_ref, sb_ref)
                stage(item, causal, g0 + u + 1, sb_ref, sa_ref)
            return carry

        lax.fori_loop(0, n_trips, trip, 0)

    own_item = lambda g, e: (jnp.minimum(g * G + e, nb - 1),) * 2
    stream(own_item, True, nb // (2 * G), 2)
    stream(lambda g, e: (tab_ref[2 * (g * G + e)], tab_ref[2 * (g * G + e) + 1]), False, n_trips, ATTN_STAGES)

    def finish(g, carry):
        for e in range(G):
            i = g * G + e
            acc = [acc_ref[i, h] for h in heads]
            o_ref[0, i] = _bf(jnp.concatenate([a[:HEAD_DIM] / a[HEAD_DIM:HEAD_DIM + 1] for a in acc], axis=0))
        return carry

    lax.fori_loop(0, nb // G, finish, 0)


def _moba_attn(k, kmean, qt, vt):
    B, H, T, _ = k.shape
    BS = MOBA_BLOCK
    nb = T // BS
    assert nb % GATE_GROUP == 0 and nb % (2 * ATTN_GROUP) == 0 and ATTN_STAGES % 2 == 0
    table, n_trips = _attn_schedule(nb)
    in_specs = [pl.BlockSpec(memory_space=pltpu.SMEM),
                pl.BlockSpec((1, 2, T, LANES), lambda b, p: (b, p, 0, 0)),
                pl.BlockSpec((1, 2, nb, LANES), lambda b, p: (b, p, 0, 0)),
                pl.BlockSpec((1, nb, LANES, BS), lambda b, p: (b, 0, p, 0)),
                pl.BlockSpec((1, nb, LANES, BS), lambda b, p: (b, 0, p, 0))]
    chains = 2 * ATTN_GROUP
    scratch = [pltpu.VMEM((nb + 1, 2, LANES, BS), BF16),
               pltpu.VMEM((nb + 1, 2, HEAD_DIM + SUM_ROWS, BS), F32),
               pltpu.VMEM((nb + 1, 2, 1, BS), F32),
               pltpu.VMEM((chains, BS, BS), F32), pltpu.VMEM((chains, BS, BS), F32)]
    return pl.pallas_call(
        functools.partial(_moba_attn_body, nb=nb, n_trips=n_trips), grid=(B, N_PAIRS), in_specs=in_specs,
        out_specs=pl.BlockSpec((1, nb, LANES, BS), lambda b, p: (b, 0, p, 0)),
        out_shape=jax.ShapeDtypeStruct((B, nb, H * HEAD_DIM, BS), BF16), scratch_shapes=scratch,
        compiler_params=_params("parallel", "parallel"), name="moba_attn")(
            jnp.asarray(table, jnp.int32), k, kmean, qt, vt)


def _moba_post_body(x_ref, ot_ref, gt_ref, wo_ref, out_ref):
    BS = MOBA_BLOCK
    for u in range(ot_ref.shape[1]):
        zt = _bf(ot_ref[0, u].astype(F32) * gt_ref[0, u].astype(F32))
        out_ref[0, u * BS:(u + 1) * BS, :] = x_ref[0, u * BS:(u + 1) * BS, :] + _mm(zt, wo_ref[...], TN)


def _moba_post(x, ot, gt, w_out, tm=512):
    B, T, D = x.shape
    big = pl.BlockSpec((1, tm, D), lambda b, i: (b, i, 0))
    tr = pl.BlockSpec((1, tm // MOBA_BLOCK, D, MOBA_BLOCK), lambda b, i: (b, i, 0, 0))
    return pl.pallas_call(
        _moba_post_body, grid=(B, T // tm), in_specs=[big, tr, tr, _full((D, D))], out_specs=big,
        out_shape=jax.ShapeDtypeStruct((B, T, D), F32),
        compiler_params=_params("parallel", "parallel"), name="moba_post")(x, ot, gt, w_out.astype(BF16))


def _rwkv_layer(x, norm_a, mu, w_in, w0, w2, a0, a2, kkw, kaw, rk, gn_w, gn_b, w_out):
    r, ld, k, v, kk, nb, bonus, g = _rwkv_pre(x, norm_a, mu, w_in, w0, w2, a0, a2, kkw, kaw, rk.reshape(-1))
    qh, yi, gm, cm = _rwkv_chunks(r, ld, k, v, kk, nb)
    return _rwkv_out(qh, yi, gm, cm, x, bonus, g, gn_w, gn_b, w_out)


def _moba_layer(x, norm_kv, w_kv, k_norm, norm_b, w_in, q_norm, w_out):
    B, T, D = x.shape
    k, kmean, qt, vt, g = _moba_pre(x, norm_kv, norm_b, w_kv, w_in, k_norm, q_norm)
    o = _moba_attn(k, kmean.transpose(0, 2, 1, 3), qt, vt)
    return _moba_post(x, o, g, w_out)


def kernel(x, norm_a, mu_a, w_in_a, w0_a, w2_a, a0_a, a2_a, kk_a, ka_a, rk_a, gn_w_a, gn_b_a, w_out_a, norm_kv, w_kv, k_norm, norm_b, w_in_b, q_norm_b, w_out_b):
    x = _rwkv_layer(x, norm_a[0], mu_a[0], w_in_a[0], w0_a[0], w2_a[0], a0_a[0], a2_a[0], kk_a[0], ka_a[0],
                    rk_a[0], gn_w_a[0], gn_b_a[0], w_out_a[0])
    return _moba_layer(x, norm_kv, w_kv, k_norm, norm_b[0], w_in_b[0], q_norm_b[0], w_out_b[0])
```

```python
import functools

import jax
import jax.numpy as jnp
from jax import lax
from jax.experimental import pallas as pl
from jax.experimental.pallas import tpu as pltpu

F32 = jnp.float32
BF16 = jnp.bfloat16

D_MODEL = 1024
HEAD_DIM = 64
N_HEADS = D_MODEL // HEAD_DIM
LANES = 128
N_PAIRS = D_MODEL // LANES
LORA = 64
MOBA_BLOCK = 256
MOBA_TOPK = 3
ROPE_THETA = 500000.0
ROPE_DIM = HEAD_DIM // 4
ROPE_HALF = ROPE_DIM // 2
RMS_EPS = 1e-6
GN_EPS = 64e-5
NEG = -1e30
LOG2E = 1.4426950408889634
DECAY_SCALE = 0.6065306597126334
SUM_ROWS = 16
TAG_LANES = LANES - HEAD_DIM
ATTN_GROUP = 4
ATTN_STAGES = 4
GATE_GROUP = 4
CHUNK = 64
VMEM_LIMIT = 56 * 1024 * 1024

NN = (((1,), (0,)), ((), ()))
NT = (((1,), (1,)), ((), ()))
TN = (((0,), (0,)), ((), ()))


def _mm(a, b, dims=NN):
    return lax.dot_general(a, b, dims, preferred_element_type=F32)


def _split(x):
    hi = x.astype(BF16)
    lo = (x - hi.astype(F32)).astype(BF16)
    return hi, lo


def _bf(x):
    return x.astype(BF16)


def _head_ones():
    r = lax.broadcasted_iota(jnp.int32, (LANES, LANES), 0) // HEAD_DIM
    c = lax.broadcasted_iota(jnp.int32, (LANES, LANES), 1) // HEAD_DIM
    return jnp.where(r == c, 1.0, 0.0).astype(BF16)


def _head_sum(x, ones):
    parts = [_mm(_bf(x[:, p * LANES:(p + 1) * LANES]), ones) for p in range(x.shape[1] // LANES)]
    return jnp.concatenate(parts, axis=1)


def _sigmoid(x):
    return 1.0 / (1.0 + jnp.exp(-x))


def _rms(x, g):
    return x * lax.rsqrt(jnp.mean(x * x, axis=-1, keepdims=True) + RMS_EPS) * g


def _params(*sem):
    return pltpu.CompilerParams(dimension_semantics=sem, vmem_limit_bytes=VMEM_LIMIT)


def _full(shape):
    return pl.BlockSpec(shape, lambda *_: (0,) * len(shape))


def _rwkv_pre_body(x_ref, xp_ref, na_ref, mu_ref, wr_ref, wwd_ref, wk_ref, wv_ref, wad_ref, wg_ref,
                   w0_ref, w2_ref, a0_ref, a2_ref, kkw_ref, kaw_ref, rk_ref,
                   r_ref, ld_ref, k_ref, v_ref, kk_ref, nb_ref, bonus_ref, g_ref):
    x = x_ref[0]
    gn = na_ref[...]
    h = _rms(x, gn)
    hp = _rms(xp_ref[0], gn)
    prev = jnp.where(pl.program_id(1) == 0, 0.0, hp[7:8, :])
    rolled = pltpu.roll(h, 1, axis=0)
    first = lax.broadcasted_iota(jnp.int32, h.shape, 0) == 0
    xx = jnp.where(first, prev, rolled) - h

    def mix(i):
        return (h + xx * mu_ref[i:i + 1, :]).astype(BF16)

    r = _mm(mix(0), wr_ref[...])
    wdn = _mm(mix(1), wwd_ref[...])
    k = _mm(mix(2), wk_ref[...])
    v = _mm(mix(3), wv_ref[...])
    adn = _mm(mix(4), wad_ref[...])
    gp = _mm(mix(5), wg_ref[...])

    ld_ref[0] = -DECAY_SCALE * _sigmoid(w0_ref[...] + _mm(jnp.tanh(wdn).astype(BF16), w2_ref[...]))
    a = _sigmoid(a0_ref[...] + _mm(adn.astype(BF16), a2_ref[...]))

    ones = _head_ones()
    kk = k * kkw_ref[...]
    kk = kk * lax.rsqrt(jnp.maximum(_head_sum(kk * kk, ones), 1e-24))
    k2 = k * (1.0 + (a - 1.0) * kaw_ref[...])
    r_ref[0] = r
    k_ref[0] = k2
    v_ref[0] = v
    kk_ref[0] = kk
    nb_ref[0] = -(kk * a)
    bonus_ref[0] = _bf(_head_sum(r * k2 * rk_ref[...], ones) * v)
    g_ref[0] = _bf(gp * _sigmoid(gp))


def _rwkv_pre(x, norm_a, mu, w_in, w0, w2, a0, a2, kkw, kaw, rk, tm=256):
    B, T, D = x.shape
    offs = [0, D, D + LORA, 2 * D + LORA, 3 * D + LORA, 3 * D + 2 * LORA, 4 * D + 2 * LORA]
    ws = [w_in[:, offs[i]:offs[i + 1]].astype(BF16) for i in range(6)]
    row = lambda a: a.reshape(1, D)
    big = pl.BlockSpec((1, tm, D), lambda b, i: (b, i, 0))
    prev = pl.BlockSpec((1, 8, D), lambda b, i: (b, jnp.maximum(i * (tm // 8) - 1, 0), 0))
    ins = [x, x, row(norm_a), mu] + ws + [row(w0), w2.astype(BF16), row(a0), a2.astype(BF16),
                                          row(kkw), row(kaw), row(rk)]
    in_specs = [big, prev] + [_full(a.shape) for a in ins[2:]]
    out = lambda dt: jax.ShapeDtypeStruct((B, T, D), dt)
    return pl.pallas_call(
        _rwkv_pre_body, grid=(B, T // tm), in_specs=in_specs, out_specs=[big] * 8,
        out_shape=[out(F32)] * 6 + [out(BF16)] * 2,
        compiler_params=_params("parallel", "parallel"), name="rwkv_pre")(*ins)


def _stack(x, lane_lo):
    return jnp.concatenate([jnp.where(lane_lo, x, 0.0), jnp.where(lane_lo, 0.0, x)], axis=0)


def _unstack(z):
    half = z.shape[0] // 2
    return z[:half] + z[half:]


def _rwkv_chunk_body(r_ref, ld_ref, k_ref, v_ref, kk_ref, nb_ref, qh_ref, yi_ref, g_ref, c_ref, *, nc):
    L = CHUNK
    S = 2 * L
    lane_lo = lax.broadcasted_iota(jnp.int32, (L, LANES), 1) < HEAD_DIM
    row = lax.broadcasted_iota(jnp.int32, (L, LANES), 0)
    pi = lax.broadcasted_iota(jnp.int32, (2 * S, 2 * S), 0)
    pj = lax.broadcasted_iota(jnp.int32, (2 * S, 2 * S), 1)
    same_head = ((pi >> 6) & 1) == ((pj >> 6) & 1)
    t_i, s_j = pi & (L - 1), pj & (L - 1)
    valid = same_head & ((s_j < t_i) | ((pi >= S) & (s_j == t_i)))
    ei = lax.broadcasted_iota(jnp.int32, (S, S), 0)
    ej = lax.broadcasted_iota(jnp.int32, (S, S), 1)
    eye = ei == ej
    stack = lambda x: _bf(_stack(x, lane_lo))

    cs = range(nc)
    each = lambda f, *xs: [f(*a) for a in zip(*xs)]
    load = lambda ref: [ref[0, c * L:(c + 1) * L, :] for c in cs]
    r, ld, k, v, kk, nb = (load(ref) for ref in (r_ref, ld_ref, k_ref, v_ref, kk_ref, nb_ref))

    def cumsum(x):
        d = 1
        while d < L:
            x = x + jnp.where(row >= d, pltpu.roll(x, d, axis=0), 0.0)
            d *= 2
        return x

    cum = each(cumsum, ld)
    cref = [x[L // 2 - 1:L // 2, :] for x in cum]
    clast = [x[L - 1:L, :] for x in cum]
    ecum = each(lambda a, b: a - b, cum, cref)
    e_in = [jnp.exp(x) for x in ecum]
    e_ex = each(lambda a, b: jnp.exp(a - b), ecum, ld)
    e_neg = [jnp.exp(-x) for x in ecum]
    p_ref = [jnp.exp(x) for x in cref]
    e_last = each(lambda a, b: jnp.exp(a - b), clast, cref)
    mul = lambda a, b: a * b
    q_m, q_n = each(mul, kk, e_ex), each(mul, r, e_in)
    k_b, k_k = each(mul, nb, e_neg), each(mul, k, e_neg)
    lhs = each(lambda a, b: jnp.concatenate([stack(a), stack(b)], axis=0), q_m, q_n)
    rhs = each(lambda a, b: jnp.concatenate([stack(a), stack(b)], axis=0), k_b, k_k)
    pm = each(lambda a, b: jnp.where(valid, _mm(a, b, NT), 0.0), lhs, rhs)
    m_b = [x[:S, :S] for x in pm]
    m_k, n_b, n_k = ([_bf(x[rs, cs_]) for x in pm] for rs, cs_ in
                     ((slice(0, S), slice(S, 2 * S)), (slice(S, 2 * S), slice(0, S)), (slice(S, 2 * S), slice(S, 2 * S))))
    tm = [jnp.where(eye, 1.0, 0.0) + x for x in m_b]
    xp = m_b
    for _ in range(5):
        xp = [_mm(xb, xb) for xb in (_bf(x) for x in xp)]
        tm = each(lambda t, x: t + _mm(_bf(t), _bf(x)), tm, xp)
    sv = [stack(x) for x in v]
    mkv = each(lambda a, b: _bf(_mm(a, b)), m_k, sv)
    qm0 = each(lambda a, b: stack(a * b), q_m, p_ref)
    wu = each(lambda t, a, b: _bf(_mm(_bf(t), jnp.concatenate([a, b], axis=1))), tm, qm0, mkv)
    kbl_t = each(lambda a, b: _bf(_stack(a * b, lane_lo).T), k_b, e_last)
    kkl_t = each(lambda a, b: _bf(_stack(a * b, lane_lo).T), k_k, e_last)
    gc = each(_mm, kbl_t, wu)
    kv = each(_mm, kkl_t, sv)
    nw = each(_mm, n_b, wu)
    nv = each(_mm, n_k, sv)
    for c in cs:
        sl = slice(c * L, (c + 1) * L)
        qh_ref[0, sl, :] = _bf(r[c] * (e_in[c] * p_ref[c]) + _unstack(nw[c][:, :S]))
        yi_ref[0, sl, :] = _bf(_unstack(nw[c][:, S:] + nv[c]))
        g_ref[0, 0, c] = _bf(_unstack(gc[c][:, :S] + jnp.where(eye, jnp.exp(clast[c]), 0.0)))
        c_ref[0, 0, c] = _bf(_unstack(gc[c][:, S:] + kv[c]))


def _rwkv_chunks(r, ld, k, v, kk, nb, nc=16):
    B, T, D = r.shape
    L = CHUNK
    tt = nc * L
    nch = T // L
    nat = pl.BlockSpec((1, tt, LANES), lambda b, p, i: (b, i, p))
    op = pl.BlockSpec((1, 1, nc, L, LANES), lambda b, p, i: (b, p, i, 0, 0))
    nat_shape = lambda dt: jax.ShapeDtypeStruct((B, T, D), dt)
    op_shape = lambda dt: jax.ShapeDtypeStruct((B, N_PAIRS, nch, L, LANES), dt)
    return pl.pallas_call(
        functools.partial(_rwkv_chunk_body, nc=nc), grid=(B, N_PAIRS, T // tt),
        in_specs=[nat] * 6, out_specs=[nat, nat, op, op],
        out_shape=[nat_shape(BF16), nat_shape(BF16), op_shape(BF16), op_shape(BF16)],
        compiler_params=_params("parallel", "parallel", "parallel"), name="rwkv_chunks")(r, ld, k, v, kk, nb)


def _rwkv_out_body(qh_ref, yi_ref, g_ref, c_ref, x_ref, bonus_ref, gate_ref, gw_ref, gb_ref, wo_ref,
                   o_ref, h_ref, y_ref, *, nc):
    L = CHUNK
    i = pl.program_id(1)
    lane_lo = lax.broadcasted_iota(jnp.int32, (L, LANES), 1) < HEAD_DIM
    cur, prev = i % 2, (i + 1) % 2

    @pl.when(i == 0)
    def _():
        h_ref[...] = jnp.zeros_like(h_ref)
        y_ref[...] = jnp.zeros_like(y_ref)

    ones = _head_ones()
    y = y_ref[prev]
    mean = _head_sum(y, ones) * (1.0 / HEAD_DIM)
    d = y - mean
    var = _head_sum(d * d, ones) * (1.0 / HEAD_DIM)
    yn = d * lax.rsqrt(var + GN_EPS) * gw_ref[...] + gb_ref[...]
    z = _bf((yn + bonus_ref[0].astype(F32)) * gate_ref[0].astype(F32))

    hs = [h_ref[p] for p in range(N_PAIRS)]
    width = 2 * D_MODEL // nc
    half = z.shape[0] // 2
    for c in range(nc):
        sl = slice(c * L, (c + 1) * L)
        nxt = []
        for p in range(N_PAIRS):
            lanes = slice(p * LANES, (p + 1) * LANES)
            lhs = jnp.concatenate([qh_ref[0, sl, lanes], _stack(g_ref[0, p, c], lane_lo)], axis=0)
            prod = _mm(lhs, _bf(hs[p]))
            y_ref[cur, sl, lanes] = prod[:L] + yi_ref[0, sl, lanes].astype(F32)
            nxt.append(prod[L:] + _stack(c_ref[0, p, c], lane_lo).astype(F32))
        hs = nxt
        rws = slice((c % 2) * half, (c % 2 + 1) * half)
        cols = slice((c // 2) * width, (c // 2 + 1) * width)
        o_ref[0, rws, cols] = x_ref[0, rws, cols] + _mm(z[rws], wo_ref[:, cols])
    for p in range(N_PAIRS):
        h_ref[p] = hs[p]


def _rwkv_out(qh, yi, g, c, x, bonus, gate, gn_w, gn_b, w_out, nc=8):
    B, T, D = qh.shape
    L = CHUNK
    tt = nc * L
    nt = T // tt
    assert nc % 2 == 0 and (2 * D) % nc == 0 and (2 * D // nc) % (2 * LANES) == 0 and tt % 16 == 0
    now = lambda b, i: (b, jnp.minimum(i, nt - 1), 0)
    lag = lambda b, i: (b, jnp.maximum(i - 1, 0), 0)
    op = pl.BlockSpec((1, N_PAIRS, nc, L, LANES), lambda b, i: (b, 0, jnp.minimum(i, nt - 1), 0, 0))
    ins = [qh, yi, g, c, x, bonus, gate, gn_w.reshape(1, D), gn_b.reshape(1, D), w_out.astype(BF16)]
    return pl.pallas_call(
        functools.partial(_rwkv_out_body, nc=nc), grid=(B, nt + 1),
        in_specs=[pl.BlockSpec((1, tt, D), now), pl.BlockSpec((1, tt, D), now), op, op,
                  pl.BlockSpec((1, tt, D), lag), pl.BlockSpec((1, tt, D), lag), pl.BlockSpec((1, tt, D), lag)]
                 + [_full(a.shape) for a in ins[7:]],
        out_specs=pl.BlockSpec((1, tt, D), lag), out_shape=jax.ShapeDtypeStruct((B, T, D), F32),
        scratch_shapes=[pltpu.VMEM((N_PAIRS, 2 * HEAD_DIM, LANES), F32), pltpu.VMEM((2, tt, D), F32)],
        compiler_params=_params("parallel", "arbitrary"), name="rwkv_out")(*ins)


def _head_norm_rope_t(xt, gain, cos, sin):
    heads = []
    for h in range(N_HEADS):
        xh = xt[h * HEAD_DIM:(h + 1) * HEAD_DIM, :]
        xh = xh * lax.rsqrt(jnp.mean(xh * xh, axis=0, keepdims=True) + RMS_EPS) * gain
        x1, x2 = xh[:ROPE_HALF], xh[ROPE_HALF:ROPE_DIM]
        heads += [x1 * cos - x2 * sin, x2 * cos + x1 * sin, xh[ROPE_DIM:]]
    return jnp.concatenate(heads, axis=0)


def _moba_pre_body(x_ref, nkv_ref, nb_ref, wkv_ref, wq_ref, wg_ref, kn_ref, qn_ref, cos_ref, sin_ref,
                   k_ref, km_ref, qt_ref, vt_ref, g_ref):
    x = x_ref[0]
    xn = x * lax.rsqrt(jnp.mean(x * x, axis=-1, keepdims=True) + RMS_EPS)
    h_kv = (xn * nkv_ref[...]).astype(BF16)
    h_b = (xn * nb_ref[...]).astype(BF16)
    cos, sin = cos_ref[...], sin_ref[...]
    tile = lambda g: jnp.concatenate([g] * (x.shape[0] // LANES), axis=1)
    kt = _head_norm_rope_t(_mm(wkv_ref[:D_MODEL, :], h_kv, NT), tile(kn_ref[...]), cos, sin)
    tag = lax.broadcasted_iota(jnp.int32, (TAG_LANES, x.shape[0]), 0) == pl.program_id(1)
    tag = jnp.where(tag, 1.0, 0.0)
    for h in range(N_HEADS):
        k_aug = jnp.concatenate([kt[h * HEAD_DIM:(h + 1) * HEAD_DIM], tag], axis=0).T
        k_ref[0, h] = k_aug.astype(BF16)
        km_ref[0, 0, h:h + 1, :] = jnp.mean(k_aug, axis=0, keepdims=True)
    qt = _head_norm_rope_t(_mm(wq_ref[...], h_b, NT), tile(qn_ref[...]), cos, sin)
    qt_ref[0, 0] = (qt * (LOG2E / HEAD_DIM ** 0.5)).astype(BF16)
    gate = _mm(wg_ref[...], h_b, NT)
    g_ref[0, 0] = _bf(gate * _sigmoid(gate))
    vt_ref[0, 0] = _bf(_mm(wkv_ref[D_MODEL:, :], h_kv, NT))


def _rope_tables_t(T):
    inv_freq = jnp.power(jnp.float32(ROPE_THETA), -jnp.arange(0, ROPE_DIM, 2, dtype=F32) / ROPE_DIM)
    ang = inv_freq[:, None] * jnp.arange(T, dtype=F32)[None, :]
    return jnp.cos(ang), jnp.sin(ang)


def _moba_pre(x, norm_kv, norm_b, w_kv, w_in, k_norm, q_norm):
    B, T, D = x.shape
    tm = MOBA_BLOCK
    nb = T // tm
    cos, sin = _rope_tables_t(T)
    bcast = lambda g: jnp.broadcast_to(g.reshape(HEAD_DIM, 1), (HEAD_DIM, LANES))
    ins = [x, norm_kv.reshape(1, D), norm_b.reshape(1, D), w_kv.T.astype(BF16), w_in[:, :D].T.astype(BF16),
           w_in[:, D:].T.astype(BF16), bcast(k_norm), bcast(q_norm), cos, sin]
    big = pl.BlockSpec((1, tm, D), lambda b, i: (b, i, 0))
    tab = pl.BlockSpec((ROPE_HALF, tm), lambda b, i: (0, i))
    tr = pl.BlockSpec((1, 1, D, tm), lambda b, i: (b, i, 0, 0))
    in_specs = [big] + [_full(a.shape) for a in ins[1:8]] + [tab, tab]
    assert tm == MOBA_BLOCK and nb <= TAG_LANES
    out_specs = [pl.BlockSpec((1, N_HEADS, tm, LANES), lambda b, i: (b, 0, i, 0)),
                 pl.BlockSpec((1, 1, N_HEADS, LANES), lambda b, i: (b, i, 0, 0)), tr, tr, tr]
    out_shape = [jax.ShapeDtypeStruct((B, N_HEADS, T, LANES), BF16), jax.ShapeDtypeStruct((B, nb, N_HEADS, LANES), F32),
                 jax.ShapeDtypeStruct((B, nb, D, tm), BF16), jax.ShapeDtypeStruct((B, nb, D, tm), BF16),
                 jax.ShapeDtypeStruct((B, nb, D, tm), BF16)]
    return pl.pallas_call(
        _moba_pre_body, grid=(B, nb), in_specs=in_specs, out_specs=out_specs, out_shape=out_shape,
        compiler_params=_params("parallel", "parallel"), name="moba_pre")(*ins)


def _attn_schedule(nb):
    items = []
    for d in range(1, nb):
        diag = [(j, j + d) for j in range(nb - d)]
        diag += [(0, nb)] * (-len(diag) % ATTN_GROUP)
        items += diag
    n_trips = -(-len(items) // (ATTN_GROUP * ATTN_STAGES))
    items += [(0, nb)] * (ATTN_GROUP * (n_trips * ATTN_STAGES + 1) - len(items))
    return [v for it in items for v in it], n_trips


def _moba_attn_body(tab_ref, k_ref, km_ref, qt_ref, vt_ref, o_ref, qa_ref, acc_ref, m_ref, sa_ref, sb_ref,
                    *, nb, n_trips):
    BS = MOBA_BLOCK
    G = ATTN_GROUP
    heads = range(2)
    ones = jnp.ones((SUM_ROWS, BS), BF16)
    zeros_tag = jnp.zeros((TAG_LANES, BS), BF16)
    rows = lambda h: slice(h * HEAD_DIM, (h + 1) * HEAD_DIM)
    v_aug = lambda j, h: jnp.concatenate([vt_ref[0, j, rows(h), :], ones], axis=0)
    k_blk = lambda j, h: k_ref[0, h, pl.ds(pl.multiple_of(j * BS, BS), BS), :]

    blk = lax.broadcasted_iota(jnp.int32, (nb, BS), 0)
    blk_f = blk.astype(F32)
    km = [_split(km_ref[0, h]) for h in heads]
    pad = [jnp.full((TAG_LANES - nb, BS), NEG, F32)] if nb < TAG_LANES else []

    def gate(g, carry):
        ih = [(g * GATE_GROUP + e, h) for e in range(GATE_GROUP) for h in heads]
        q = [qt_ref[0, i, rows(h), :] for i, h in ih]
        qz = [jnp.concatenate([x, zeros_tag], axis=0) for x in q]
        gs = [_mm(km[h][0], x) + _mm(km[h][1], x) for x, (i, h) in zip(qz, ih)]
        work = [jnp.where(blk < i, x, -jnp.inf) for x, (i, h) in zip(gs, ih)]
        f = [jnp.full((nb, BS), NEG, F32) for _ in ih]
        for _ in range(MOBA_TOPK):
            mx = [jnp.max(x, axis=0, keepdims=True) for x in work]
            first = [jnp.min(jnp.where(w == m, blk_f, float(nb)), axis=0, keepdims=True) for w, m in zip(work, mx)]
            hit = [blk_f == x for x in first]
            f = [jnp.where(h_, jnp.where(m > -jnp.inf, 0.0, f_), f_) for h_, m, f_ in zip(hit, mx, f)]
            work = [jnp.where(h_, -jnp.inf, w) for h_, w in zip(hit, work)]
        for x, f_, (i, h) in zip(q, f, ih):
            qa_ref[i, h] = jnp.concatenate([x, _bf(jnp.concatenate([f_] + pad, axis=0))], axis=0)
        return carry

    lax.fori_loop(0, nb // GATE_GROUP, gate, 0)
    for h in heads:
        qa_ref[nb, h] = jnp.concatenate([jnp.zeros((HEAD_DIM, BS), BF16), jnp.full((TAG_LANES, BS), NEG, BF16)], axis=0)
        acc_ref[nb, h] = jnp.zeros(acc_ref.shape[2:], F32)
        m_ref[nb, h] = jnp.full(m_ref.shape[2:], 0.5 * NEG, F32)

    key_pos = lax.broadcasted_iota(jnp.int32, (BS, BS), 0)
    qry_pos = lax.broadcasted_iota(jnp.int32, (BS, BS), 1)
    eh = [(e, h) for e in range(G) for h in heads]

    def scores_one(item, causal, g, dst_ref, c, e, h):
        j, i = item(g, e)
        q_op = jnp.concatenate([qt_ref[0, i, rows(h), :], zeros_tag], axis=0) if causal else qa_ref[i, h]
        dst_ref[c] = _mm(k_blk(j, h), q_op)

    def consume_one(item, causal, src_ref, g, c, e, h):
        j, i = item(g, e)
        if causal:
            s = jnp.where(key_pos <= qry_pos, src_ref[c], NEG)
            m_new = jnp.max(s, axis=0, keepdims=True)
            acc = _mm(v_aug(j, h), _bf(jnp.exp2(s - m_new)))
        else:
            s = src_ref[c]
            m_old = m_ref[i, h]
            m_new = jnp.maximum(m_old, jnp.max(s, axis=0, keepdims=True))
            acc = jnp.exp2(m_old - m_new) * acc_ref[i, h] + _mm(v_aug(j, h), _bf(jnp.exp2(s - m_new)))
        m_ref[i, h] = m_new
        acc_ref[i, h] = acc

    def stage(item, causal, g, src_ref, dst_ref):
        for c, (e, h) in enumerate(eh):
            scores_one(item, causal, g + 1, dst_ref, c, e, h)
            consume_one(item, causal, src_ref, g, c, e, h)

    def stream(item, causal, n_trips, stages):
        for c, (e, h) in enumerate(eh):
            scores_one(item, causal, 0, sa_ref, c, e, h)

        def trip(t, carry):
            g0 = t * stages
            for u in range(0, stages, 2):
                stage(item, causal, g0 + u, sa_ref, sb_ref)
                stage(item, causal, g0 + u + 1, sb_ref, sa_ref)
            return carry

        lax.fori_loop(0, n_trips, trip, 0)

    own_item = lambda g, e: (jnp.minimum(g * G + e, nb - 1),) * 2
    stream(own_item, True, nb // (2 * G), 2)
    stream(lambda g, e: (tab_ref[2 * (g * G + e)], tab_ref[2 * (g * G + e) + 1]), False, n_trips, ATTN_STAGES)

    def finish(g, carry):
        for e in range(G):
            i = g * G + e
            acc = [acc_ref[i, h] for h in heads]
            o_ref[0, i] = _bf(jnp.concatenate([a[:HEAD_DIM] / a[HEAD_DIM:HEAD_DIM + 1] for a in acc], axis=0))
        return carry

    lax.fori_loop(0, nb // G, finish, 0)


def _moba_attn(k, kmean, qt, vt):
    B, H, T, _ = k.shape
    BS = MOBA_BLOCK
    nb = T // BS
    assert nb % GATE_GROUP == 0 and nb % (2 * ATTN_GROUP) == 0 and ATTN_STAGES % 2 == 0
    table, n_trips = _attn_schedule(nb)
    in_specs = [pl.BlockSpec(memory_space=pltpu.SMEM),
                pl.BlockSpec((1, 2, T, LANES), lambda b, p: (b, p, 0, 0)),
                pl.BlockSpec((1, 2, nb, LANES), lambda b, p: (b, p, 0, 0)),
                pl.BlockSpec((1, nb, LANES, BS), lambda b, p: (b, 0, p, 0)),
                pl.BlockSpec((1, nb, LANES, BS), lambda b, p: (b, 0, p, 0))]
    chains = 2 * ATTN_GROUP
    scratch = [pltpu.VMEM((nb + 1, 2, LANES, BS), BF16),
               pltpu.VMEM((nb + 1, 2, HEAD_DIM + SUM_ROWS, BS), F32),
               pltpu.VMEM((nb + 1, 2, 1, BS), F32),
               pltpu.VMEM((chains, BS, BS), F32), pltpu.VMEM((chains, BS, BS), F32)]
    return pl.pallas_call(
        functools.partial(_moba_attn_body, nb=nb, n_trips=n_trips), grid=(B, N_PAIRS), in_specs=in_specs,
        out_specs=pl.BlockSpec((1, nb, LANES, BS), lambda b, p: (b, 0, p, 0)),
        out_shape=jax.ShapeDtypeStruct((B, nb, H * HEAD_DIM, BS), BF16), scratch_shapes=scratch,
        compiler_params=_params("parallel", "parallel"), name="moba_attn")(
            jnp.asarray(table, jnp.int32), k, kmean, qt, vt)


def _moba_post_body(x_ref, ot_ref, gt_ref, wo_ref, out_ref):
    BS = MOBA_BLOCK
    for u in range(ot_ref.shape[1]):
        zt = _bf(ot_ref[0, u].astype(F32) * gt_ref[0, u].astype(F32))
        out_ref[0, u * BS:(u + 1) * BS, :] = x_ref[0, u * BS:(u + 1) * BS, :] + _mm(zt, wo_ref[...], TN)


def _moba_post(x, ot, gt, w_out, tm=512):
    B, T, D = x.shape
    big = pl.BlockSpec((1, tm, D), lambda b, i: (b, i, 0))
    tr = pl.BlockSpec((1, tm // MOBA_BLOCK, D, MOBA_BLOCK), lambda b, i: (b, i, 0, 0))
    return pl.pallas_call(
        _moba_post_body, grid=(B, T // tm), in_specs=[big, tr, tr, _full((D, D))], out_specs=big,
        out_shape=jax.ShapeDtypeStruct((B, T, D), F32),
        compiler_params=_params("parallel", "parallel"), name="moba_post")(x, ot, gt, w_out.astype(BF16))


def _rwkv_layer(x, norm_a, mu, w_in, w0, w2, a0, a2, kkw, kaw, rk, gn_w, gn_b, w_out):
    r, ld, k, v, kk, nb, bonus, g = _rwkv_pre(x, norm_a, mu, w_in, w0, w2, a0, a2, kkw, kaw, rk.reshape(-1))
    qh, yi, gm, cm = _rwkv_chunks(r, ld, k, v, kk, nb)
    return _rwkv_out(qh, yi, gm, cm, x, bonus, g, gn_w, gn_b, w_out)


def _moba_layer(x, norm_kv, w_kv, k_norm, norm_b, w_in, q_norm, w_out):
    B, T, D = x.shape
    k, kmean, qt, vt, g = _moba_pre(x, norm_kv, norm_b, w_kv, w_in, k_norm, q_norm)
    o = _moba_attn(k, kmean.transpose(0, 2, 1, 3), qt, vt)
    return _moba_post(x, o, g, w_out)


def kernel(x, norm_a, mu_a, w_in_a, w0_a, w2_a, a0_a, a2_a, kk_a, ka_a, rk_a, gn_w_a, gn_b_a, w_out_a, norm_kv, w_kv, k_norm, norm_b, w_in_b, q_norm_b, w_out_b):
    x = _rwkv_layer(x, norm_a[0], mu_a[0], w_in_a[0], w0_a[0], w2_a[0], a0_a[0], a2_a[0], kk_a[0], ka_a[0],
                    rk_a[0], gn_w_a[0], gn_b_a[0], w_out_a[0])
    return _moba_layer(x, norm_kv, w_kv, k_norm, norm_b[0], w_in_b[0], q_norm_b[0], w_out_b[0])
```
